```python
import jax, jax.numpy as jnp
from jax import lax
import numpy as np

D_MODEL = 1024
BATCH = 16
SEQ = 4096
DEPTH = 2

CHUNK = 64
EPS = 1e-6
D_RNN = 1024
RNN_BLOCKS = 8
RNN_BW = D_RNN // RNN_BLOCKS
CONV_W = 4
RG_C = 8.0
GLA_HEADS = 4
GLA_DK = 128
GLA_DV = 256
GLA_RANK = 16
GLA_TAU = 16.0
N_EXPERTS = 128
TOP_K = 8
D_EXPERT = 256
D_SHARED = 256
ROUTED_SCALE = 2.5
MOE_BLOCK = 64
IN_SPLITS = (D_RNN, D_RNN, GLA_HEADS * GLA_DK, GLA_HEADS * GLA_DK, GLA_HEADS * GLA_DV, GLA_HEADS * GLA_DV, GLA_RANK, D_MODEL, D_MODEL)
N_IN = 2 * D_RNN + 2 * GLA_HEADS * GLA_DK + 2 * GLA_HEADS * GLA_DV + GLA_RANK + 2 * D_MODEL

kernel_name = 'hybrid_rglru_gla_moe_adaln_encoder'


def _rmsnorm(x, g):
    xf = x.astype(jnp.float32)
    y = xf * lax.rsqrt(jnp.mean(xf * xf, axis=-1, keepdims=True) + EPS)
    return (y * g.astype(jnp.float32)).astype(x.dtype)


def _swiglu(x, w_gu, w_down):
    g, up = jnp.split(x @ w_gu, 2, axis=-1)
    return (jax.nn.silu(g) * up) @ w_down


def _causal_conv(x, w, b):
    s = x.shape[1]
    xp = jnp.pad(x, ((0, 0), (CONV_W - 1, 0), (0, 0)))
    y = b + xp[:, 0:s] * w[0]
    for j in range(1, CONV_W):
        y = y + xp[:, j:j + s] * w[j]
    return y


def _rglru(x, w_a, b_a, w_x, b_x, lam):
    bsz, s, _ = x.shape
    xb = x.reshape(bsz, s, RNN_BLOCKS, RNN_BW)
    r = jax.nn.sigmoid(jnp.einsum('bshi,hij->bshj', xb, w_a).reshape(bsz, s, D_RNN) + b_a)
    i = jax.nn.sigmoid(jnp.einsum('bshi,hij->bshj', xb, w_x).reshape(bsz, s, D_RNN) + b_x)
    log_a = -RG_C * r.astype(jnp.float32) * jax.nn.softplus(-lam.astype(jnp.float32))
    a = jnp.exp(log_a)
    mult = jnp.sqrt(-jnp.expm1(2.0 * log_a))
    mult = jnp.where((jnp.arange(s) == 0)[None, :, None], 1.0, mult)
    bt = mult * (i * x).astype(jnp.float32)

    def combine(p, q):
        return p[0] * q[0], q[0] * p[1] + q[1]

    _, h = lax.associative_scan(combine, (a, bt), axis=1)
    return h.astype(x.dtype)


def _gla(q, k, v, log_alpha):
    bsz, s = q.shape[:2]
    nc = s // CHUNK
    f32 = jnp.float32

    def chunks(t):
        return t.astype(f32).reshape(bsz, nc, CHUNK, *t.shape[2:]).swapaxes(0, 1)

    cum = jnp.cumsum(chunks(log_alpha), axis=2)
    tot = cum[:, :, -1]
    k_dec = chunks(k) * jnp.exp(tot[:, :, None] - cum)

    def step(state, inp):
        qc, kc, vc, tc = inp
        state = jnp.exp(tc)[..., None] * state + jnp.einsum('bchk,bchv->bhkv', kc, vc)
        return state, jnp.einsum('bchk,bhkv->bchv', qc, state)

    s0 = jnp.zeros((bsz, GLA_HEADS, GLA_DK, GLA_DV), f32)
    _, o = lax.scan(step, s0, (chunks(q), k_dec, chunks(v), tot))
    return o.swapaxes(0, 1).reshape(bsz, s, GLA_HEADS, GLA_DV).astype(q.dtype)


def _mixer(u, w_in, conv_w, conv_b, rg_wa, rg_ba, rg_wx, rg_bx, rg_lambda,
           gla_w_up, gla_b_up, gla_norm_g, w_branch_a, w_branch_b, w_out):
    bsz, s, _ = u.shape
    w_rx, w_rg, w_q, w_k, w_v, w_og, w_lr, w_ga, w_gb = jnp.split(
        w_in, np.cumsum(IN_SPLITS)[:-1].tolist(), axis=1)
    xr = _causal_conv(u @ w_rx, conv_w, conv_b)
    h_rnn = _rglru(xr, rg_wa, rg_ba, rg_wx, rg_bx, rg_lambda)
    y_a = (h_rnn * jax.nn.gelu(u @ w_rg, approximate=True)) @ w_branch_a
    q = (u @ w_q).reshape(bsz, s, GLA_HEADS, GLA_DK) * (GLA_DK ** -0.5)
    k = (u @ w_k).reshape(bsz, s, GLA_HEADS, GLA_DK)
    v = (u @ w_v).reshape(bsz, s, GLA_HEADS, GLA_DV)
    gate_logit = ((u @ w_lr) @ gla_w_up + gla_b_up).astype(jnp.float32)
    log_alpha = (jax.nn.log_sigmoid(gate_logit) / GLA_TAU).reshape(bsz, s, GLA_HEADS, GLA_DK)
    o = _gla(q, k, v, log_alpha)
    o = _rmsnorm(o, gla_norm_g).reshape(bsz, s, GLA_HEADS * GLA_DV) * jax.nn.silu(u @ w_og)
    y_b = o @ w_branch_b
    m = jax.nn.sigmoid(u @ w_ga) * y_a + jax.nn.sigmoid(u @ w_gb) * y_b
    return m @ w_out


def _routed_experts(u, idx, wts, w_gu, w_down):
    s, d = u.shape
    m = s * TOP_K
    nb = -(-(m + N_EXPERTS * (MOE_BLOCK - 1)) // MOE_BLOCK)
    p = nb * MOE_BLOCK
    e_flat = idx.reshape(m)
    order = jnp.argsort(e_flat, stable=True)
    e_sorted = e_flat[order]
    tok_sorted = (order // TOP_K).astype(jnp.int32)
    w_sorted = wts.reshape(m)[order]
    counts = jnp.bincount(e_flat, length=N_EXPERTS)
    padded = (counts + MOE_BLOCK - 1) // MOE_BLOCK * MOE_BLOCK
    start = jnp.cumsum(counts) - counts
    ends_p = jnp.cumsum(padded)
    pstart = ends_p - padded
    dest = pstart[e_sorted] + jnp.arange(m) - start[e_sorted]
    row_tok = jnp.full((p,), s, jnp.int32).at[dest].set(tok_sorted)
    row_w = jnp.zeros((p,), wts.dtype).at[dest].set(w_sorted)
    block_exp = jnp.minimum(jnp.searchsorted(ends_p, jnp.arange(nb) * MOE_BLOCK, side='right'), N_EXPERTS - 1)
    u_pad = jnp.concatenate([u, jnp.zeros((1, d), u.dtype)], axis=0)[row_tok].reshape(nb, MOE_BLOCK, d)

    def group_fn(args):
        xg, e = args
        return _swiglu(xg, w_gu[e], w_down[e])

    y = lax.map(group_fn, (u_pad, block_exp)).reshape(p, d)
    return jnp.zeros((s + 1, d), u.dtype).at[row_tok].add(y * row_w[:, None])[:s]


def _moe(u, router_w, router_b, exp_w_gu, exp_w_down, shared_w_gu, shared_w_down):
    scores = jax.nn.sigmoid((u @ router_w).astype(jnp.float32))
    _, idx = lax.top_k(scores + router_b.astype(jnp.float32), TOP_K)
    s_sel = jnp.take_along_axis(scores, idx, axis=-1)
    wts = (s_sel / jnp.sum(s_sel, axis=-1, keepdims=True) * ROUTED_SCALE).astype(u.dtype)
    routed = lax.map(lambda a: _routed_experts(a[0], a[1], a[2], exp_w_gu, exp_w_down), (u, idx, wts))
    return routed + _swiglu(u, shared_w_gu, shared_w_down)


def setup_inputs(seed: int = 0) -> dict:
    key = jax.random.key(seed)
    ks = jax.random.split(key, 32)
    L, D = DEPTH, D_MODEL

    def nrm(k, shape, scale):
        return jax.random.normal(k, shape, jnp.float32) * scale

    def gain(k, shape):
        return 1.0 + 0.02 * jax.random.normal(k, shape, jnp.float32)

    a_pow = jax.random.uniform(ks[12], (L, D_RNN), jnp.float32, minval=0.9, maxval=0.999)
    a0 = a_pow ** (1.0 / RG_C)
    rg_lambda = jnp.log(a0) - jnp.log1p(-a0)
    return {
        'x': nrm(ks[0], (BATCH, SEQ, D), 1.0),
        'c': nrm(ks[1], (BATCH, D), 1.0),
        'ada_w': nrm(ks[2], (L, D, 6 * D), 0.5 * D ** -0.5),
        'ada_b': nrm(ks[3], (L, 6 * D), 0.02),
        'norm1_g': gain(ks[4], (L, D)),
        'w_in': nrm(ks[5], (L, D, N_IN), D ** -0.5),
        'conv_w': nrm(ks[6], (L, CONV_W, D_RNN), CONV_W ** -0.5),
        'conv_b': nrm(ks[7], (L, D_RNN), 0.02),
        'rg_wa': nrm(ks[8], (L, RNN_BLOCKS, RNN_BW, RNN_BW), RNN_BW ** -0.5),
        'rg_ba': nrm(ks[9], (L, D_RNN), 0.02),
        'rg_wx': nrm(ks[10], (L, RNN_BLOCKS, RNN_BW, RNN_BW), RNN_BW ** -0.5),
        'rg_bx': nrm(ks[11], (L, D_RNN), 0.02),
        'rg_lambda': rg_lambda,
        'gla_w_up': nrm(ks[13], (L, GLA_RANK, GLA_HEADS * GLA_DK), GLA_RANK ** -0.5),
        'gla_b_up': nrm(ks[14], (L, GLA_HEADS * GLA_DK), 0.02),
        'gla_norm_g': gain(ks[15], (L, GLA_HEADS, GLA_DV)),
        'w_branch_a': nrm(ks[16], (L, D_RNN, D), D_RNN ** -0.5),
        'w_branch_b': nrm(ks[17], (L, GLA_HEADS * GLA_DV, D), (GLA_HEADS * GLA_DV) ** -0.5),
        'w_out': nrm(ks[18], (L, D, D), D ** -0.5),
        'norm2_g': gain(ks[19], (L, D)),
        'router_w': nrm(ks[20], (L, D, N_EXPERTS), D ** -0.5),
        'router_b': nrm(ks[21], (L, N_EXPERTS), 0.01),
        'exp_w_gu': nrm(ks[22], (L, N_EXPERTS, D, 2 * D_EXPERT), D ** -0.5),
        'exp_w_down': nrm(ks[23], (L, N_EXPERTS, D_EXPERT, D), D_EXPERT ** -0.5),
        'shared_w_gu': nrm(ks[24], (L, D, 2 * D_SHARED), D ** -0.5),
        'shared_w_down': nrm(ks[25], (L, D_SHARED, D), D_SHARED ** -0.5),
        'final_g': gain(ks[26], (D,)),
    }


def reference(x, c, ada_w, ada_b, norm1_g, w_in, conv_w, conv_b, rg_wa, rg_ba, rg_wx, rg_bx,
              rg_lambda, gla_w_up, gla_b_up, gla_norm_g, w_branch_a, w_branch_b, w_out, norm2_g,
              router_w, router_b, exp_w_gu, exp_w_down, shared_w_gu, shared_w_down, final_g):
    h = x
    c_act = jax.nn.silu(c)
    for l in range(DEPTH):
        mod = (c_act @ ada_w[l] + ada_b[l])[:, None, :]
        sh1, sc1, g1, sh2, sc2, g2 = jnp.split(mod, 6, axis=-1)
        u = _rmsnorm(h, norm1_g[l]) * (1 + sc1) + sh1
        h = h + g1 * _mixer(u, w_in[l], conv_w[l], conv_b[l], rg_wa[l], rg_ba[l], rg_wx[l], rg_bx[l],
                            rg_lambda[l], gla_w_up[l], gla_b_up[l], gla_norm_g[l],
                            w_branch_a[l], w_branch_b[l], w_out[l])
        u = _rmsnorm(h, norm2_g[l]) * (1 + sc2) + sh2
        h = h + g2 * _moe(u, router_w[l], router_b[l], exp_w_gu[l], exp_w_down[l],
                          shared_w_gu[l], shared_w_down[l])
    return _rmsnorm(h, final_g)
```

```python
import functools

import jax
import jax.numpy as jnp
import numpy as np
from jax import lax
from jax.experimental import pallas as pl
from jax.experimental.pallas import tpu as pltpu

EPS = 1e-6
CHUNK = 64
CONV_W = 4
RG_C = 8.0
RNN_BLOCKS = 8
GLA_HEADS = 4
GLA_DK = 128
GLA_DV = 256
GLA_RANK = 16
GLA_TAU = 16.0
N_EXPERTS = 128
TOP_K = 8
D_EXPERT = 256
ROUTED_SCALE = 2.5

LANES = 128
SUBLANES = 8
VMEM_LIMIT = 56 * 1024 * 1024

TM_IN = 256
TS_RNN = 256
TS_GLA = 256
TM_POST = 256
TN_DEST = 2048
TD_DISP = 256
R_EXP = 256
TC_COMB = 128

bf16 = jnp.bfloat16
f32 = jnp.float32


def _cparams(n_axes):
    return pltpu.CompilerParams(
        dimension_semantics=("arbitrary",) * n_axes,
        vmem_limit_bytes=VMEM_LIMIT,
    )


def _dot(a, b):
    return jnp.dot(a, b, preferred_element_type=f32)


def _dot_nt(a, b):
    return lax.dot_general(a, b, (((1,), (1,)), ((), ())), preferred_element_type=f32)


def _dot_tn(a, b):
    return lax.dot_general(a, b, (((0,), (0,)), ((), ())), preferred_element_type=f32)


def _split_bf16(x):
    hi = x.astype(bf16)
    lo = (x - hi.astype(f32)).astype(bf16)
    return hi, lo


def _rmsnorm_rows(x, g):
    return x * lax.rsqrt(jnp.mean(x * x, axis=-1, keepdims=True) + EPS) * g


def _log_sigmoid(x):
    return jnp.minimum(x, 0.0) - jnp.log1p(jnp.exp(-jnp.abs(x)))


def _softplus(x):
    return jnp.maximum(x, 0.0) + jnp.log1p(jnp.exp(-jnp.abs(x)))


def _ada_kernel(c_ref, w_ref, b_ref, o_ref):
    c_act = jax.nn.silu(c_ref[...])
    c_hi, c_lo = _split_bf16(c_act)
    w_hi, w_lo = _split_bf16(w_ref[...])
    acc = _dot(c_hi, w_hi) + _dot(c_lo, w_hi) + _dot(c_hi, w_lo)
    o_ref[...] = acc + b_ref[...]


def _ada(c, ada_w, ada_b):
    n_layers, d, n6 = ada_w.shape
    bsz = c.shape[0]
    tn = 512
    return pl.pallas_call(
        _ada_kernel,
        grid=(n_layers, n6 // tn),
        in_specs=[
            pl.BlockSpec((bsz, d), lambda l, j: (0, 0)),
            pl.BlockSpec((None, d, tn), lambda l, j: (l, 0, j)),
            pl.BlockSpec((None, 1, tn), lambda l, j: (l, 0, j)),
        ],
        out_specs=pl.BlockSpec((None, bsz, tn), lambda l, j: (l, 0, j)),
        out_shape=jax.ShapeDtypeStruct((n_layers, bsz, n6), f32),
        compiler_params=_cparams(2),
        name="ada_mod",
    )(c, ada_w, ada_b.reshape(n_layers, 1, n6))


def _in_kernel(h_ref, mod_ref, g_ref, w_ref, wlr_ref, wup_ref, bup_ref,
               xr_ref, gg_ref, q_ref, k_ref, v_ref, og_ref, ga_ref, gb_ref, la_ref, u_scr):
    d = h_ref.shape[-1]
    sh1 = mod_ref[:, 0:d]
    sc1 = mod_ref[:, d:2 * d]
    u = _rmsnorm_rows(h_ref[...], g_ref[...]) * (1.0 + sc1) + sh1
    u_scr[...] = u.astype(bf16)

    def proj(c0, width):
        return _dot(u_scr[...], w_ref[:, c0:c0 + width])

    cw = 512
    for c in range(d // cw):
        xr_ref[:, c * cw:(c + 1) * cw] = proj(c * cw, cw)
    for c in range(d // cw):
        y = proj(d + c * cw, cw)
        gg_ref[:, c * cw:(c + 1) * cw] = jax.nn.gelu(y, approximate=True).astype(bf16)
    dqk = GLA_HEADS * GLA_DK
    q_ref[...] = (proj(2 * d, dqk) * (GLA_DK ** -0.5)).astype(bf16)
    k_ref[...] = proj(2 * d + dqk, dqk).astype(bf16)
    base = 2 * d + 2 * dqk
    for c in range(d // cw):
        v_ref[:, c * cw:(c + 1) * cw] = proj(base + c * cw, cw).astype(bf16)
    base += d
    for c in range(d // cw):
        y = proj(base + c * cw, cw)
        og_ref[:, c * cw:(c + 1) * cw] = jax.nn.silu(y).astype(bf16)
    base += d
    for c in range(d // cw):
        y = proj(base + c * cw, cw)
        ga_ref[:, c * cw:(c + 1) * cw] = jax.nn.sigmoid(y).astype(bf16)
    base += d
    for c in range(d // cw):
        y = proj(base + c * cw, cw)
        gb_ref[:, c * cw:(c + 1) * cw] = jax.nn.sigmoid(y).astype(bf16)
    lr = _dot(u_scr[...], wlr_ref[...]).astype(bf16)
    logit = _dot(lr, wup_ref[...]) + bup_ref[...]
    la_ref[...] = _log_sigmoid(logit) * (1.0 / GLA_TAU)


def _in_proj(h, mod, norm_g, w_main, w_lr, w_up, b_up):
    bsz, s, d = h.shape
    tm = TM_IN
    dqk = GLA_HEADS * GLA_DK
    row = lambda width: pl.BlockSpec((None, tm, width), lambda b, i: (b, i, 0))
    full = lambda a: pl.BlockSpec(a.shape, lambda b, i: (0,) * a.ndim)
    out_shapes = [
        jax.ShapeDtypeStruct((bsz, s, d), f32),
        jax.ShapeDtypeStruct((bsz, s, d), bf16),
        jax.ShapeDtypeStruct((bsz, s, dqk), bf16),
        jax.ShapeDtypeStruct((bsz, s, dqk), bf16),
        jax.ShapeDtypeStruct((bsz, s, d), bf16),
        jax.ShapeDtypeStruct((bsz, s, d), bf16),
        jax.ShapeDtypeStruct((bsz, s, d), bf16),
        jax.ShapeDtypeStruct((bsz, s, d), bf16),
        jax.ShapeDtypeStruct((bsz, s, dqk), f32),
    ]
    return pl.pallas_call(
        _in_kernel,
        grid=(bsz, s // tm),
        in_specs=[
            row(d),
            pl.BlockSpec((None, 1, mod.shape[-1]), lambda b, i: (b, 0, 0)),
            full(norm_g),
            pl.BlockSpec(w_main.shape, lambda b, i: (0, 0), pipeline_mode=pl.Buffered(1)),
            full(w_lr), full(w_up), full(b_up),
        ],
        out_specs=[row(d), row(d), row(dqk), row(dqk), row(d), row(d), row(d), row(d), row(dqk)],
        out_shape=out_shapes,
        scratch_shapes=[pltpu.VMEM((tm, d), bf16)],
        compiler_params=_cparams(2),
        name="in_proj",
    )(h, mod, norm_g, w_main, w_lr, w_up, b_up)


def _rglru_kernel(xr_ref, gg_ref, cw_ref, cb_ref, wab_ref, ba_ref, bx_ref, lam_ref,
                  o_ref, xpad, hcar):
    ts, d = xr_ref.shape
    si = pl.program_id(1)

    @pl.when(si == 0)
    def _():
        xpad[0:SUBLANES, :] = jnp.zeros((SUBLANES, d), f32)
        hcar[...] = jnp.zeros_like(hcar)

    xpad[SUBLANES:SUBLANES + ts, :] = xr_ref[...]
    xc = cb_ref[...] + xpad[SUBLANES:SUBLANES + ts, :] * cw_ref[CONV_W - 1:CONV_W, :]
    for j in range(CONV_W - 1):
        off = SUBLANES - (CONV_W - 1) + j
        xc = xc + xpad[off:off + ts, :] * cw_ref[j:j + 1, :]
    xpad[0:SUBLANES, :] = xpad[ts:ts + SUBLANES, :]

    groups = ts // SUBLANES
    bw = d // RNN_BLOCKS
    sub = lax.broadcasted_iota(jnp.int32, (groups, SUBLANES, bw), 1)
    first_row = (lax.broadcasted_iota(jnp.int32, (ts, bw), 0) == 0) & (si == 0)
    sp = _softplus(-lam_ref[...])
    xcb = xc.astype(bf16)
    for hb in range(RNN_BLOCKS):
        cs = slice(hb * bw, (hb + 1) * bw)
        x_h = xc[:, cs]
        g = _dot(xcb[:, cs], wab_ref[hb])
        r = jax.nn.sigmoid(g[:, :bw] + ba_ref[:, cs])
        i = jax.nn.sigmoid(g[:, bw:] + bx_ref[:, cs])
        log_a = (-RG_C) * r * sp[:, cs]
        a = jnp.exp(log_a)
        mult = jnp.sqrt(1.0 - a * a)
        mult = jnp.where(first_row, 1.0, mult)
        bt = mult * (i * x_h)
        av = a.reshape(groups, SUBLANES, bw)
        bv = bt.reshape(groups, SUBLANES, bw)
        for sh in (1, 2, 4):
            a_prev = pltpu.roll(av, sh, 1)
            b_prev = pltpu.roll(bv, sh, 1)
            keep = sub >= sh
            bv = jnp.where(keep, av * b_prev + bv, bv)
            av = jnp.where(keep, av * a_prev, av)
        hin = hcar[0:1, cs]
        outs = []
        for gi in range(groups):
            hg = av[gi] * hin + bv[gi]
            outs.append(hg)
            hin = hg[SUBLANES - 1:SUBLANES, :]
        hcar[0:1, cs] = hin
        hs = jnp.concatenate(outs, axis=0)
        o_ref[:, cs] = (hs * gg_ref[:, cs].astype(f32)).astype(bf16)


def _rglru(xr, gg, conv_w, conv_b, wab, ba, bx, lam):
    bsz, s, d = xr.shape
    ts = TS_RNN
    row = pl.BlockSpec((None, ts, d), lambda b, i: (b, i, 0))
    full = lambda a: pl.BlockSpec(a.shape, lambda b, i: (0,) * a.ndim)
    return pl.pallas_call(
        _rglru_kernel,
        grid=(bsz, s // ts),
        in_specs=[row, row, full(conv_w), full(conv_b), full(wab), full(ba), full(bx), full(lam)],
        out_specs=row,
        out_shape=jax.ShapeDtypeStruct((bsz, s, d), bf16),
        scratch_shapes=[pltpu.VMEM((ts + SUBLANES, d), f32), pltpu.VMEM((SUBLANES, d), f32)],
        compiler_params=_cparams(2),
        name="rglru",
    )(xr, gg, conv_w, conv_b, wab, ba, bx, lam)


def _gla_kernel(q_ref, k_ref, v_ref, la_ref, og_ref, g_ref, o_ref, st):
    ts = q_ref.shape[0]
    si = pl.program_id(1)

    @pl.when(si == 0)
    def _():
        st[...] = jnp.zeros_like(st)

    ri = lax.broadcasted_iota(jnp.int32, (CHUNK, CHUNK), 0)
    ci = lax.broadcasted_iota(jnp.int32, (CHUNK, CHUNK), 1)
    tri = jnp.where(ci <= ri, 1.0, 0.0).astype(bf16)
    for c in range(ts // CHUNK):
        rs = slice(c * CHUNK, (c + 1) * CHUNK)
        la_hi, la_lo = _split_bf16(la_ref[rs, :])
        cum = _dot(tri, la_hi) + _dot(tri, la_lo)
        tot = cum[CHUNK - 1:CHUNK, :]
        kdec = (k_ref[rs, :].astype(f32) * jnp.exp(tot - cum)).astype(bf16)
        dec = jnp.exp(tot)
        for hd in range(GLA_HEADS):
            ks = slice(hd * GLA_DK, (hd + 1) * GLA_DK)
            vs = slice(hd * GLA_DV, (hd + 1) * GLA_DV)
            s_new = st[hd] * dec[:, ks] + _dot_tn(v_ref[rs, vs], kdec[:, ks])
            st[hd] = s_new
            o = _dot_nt(q_ref[rs, ks], s_new.astype(bf16))
            o = _rmsnorm_rows(o, g_ref[:, vs])
            o_ref[rs, vs] = (o * og_ref[rs, vs].astype(f32)).astype(bf16)


def _gla(q, k, v, la, og, norm_g):
    bsz, s, dqk = q.shape
    d = v.shape[-1]
    ts = TS_GLA
    row = lambda width: pl.BlockSpec((None, ts, width), lambda b, i: (b, i, 0))
    return pl.pallas_call(
        _gla_kernel,
        grid=(bsz, s // ts),
        in_specs=[row(dqk), row(dqk), row(d), row(dqk), row(d),
                  pl.BlockSpec(norm_g.shape, lambda b, i: (0, 0))],
        out_specs=row(d),
        out_shape=jax.ShapeDtypeStruct((bsz, s, d), bf16),
        scratch_shapes=[pltpu.VMEM((GLA_HEADS, GLA_DV, GLA_DK), f32)],
        compiler_params=_cparams(2),
        name="gla",
    )(q, k, v, la, og, norm_g)


def _post_kernel(ya_ref, ob_ref, ga_ref, gb_ref, h_ref, mod_ref, wa_ref, wb_ref, wo_ref, g2_ref,
                 wsgu_ref, wsd_ref, wrh_ref, wrl_ref, rb_ref,
                 hmid_ref, urow_ref, idx_ref, wts_ref, rank_ref, cnt_ref, cnt):
    tm, d = h_ref.shape
    first = (pl.program_id(0) == 0) & (pl.program_id(1) == 0)

    @pl.when(first)
    def _():
        cnt[...] = jnp.zeros_like(cnt)

    g1 = mod_ref[:, 2 * d:3 * d]
    sh2 = mod_ref[:, 3 * d:4 * d]
    sc2 = mod_ref[:, 4 * d:5 * d]
    g2 = mod_ref[:, 5 * d:6 * d]

    ya = _dot(ya_ref[...], wa_ref[...])
    yb = _dot(ob_ref[...], wb_ref[...])
    m = (ga_ref[...].astype(f32) * ya + gb_ref[...].astype(f32) * yb).astype(bf16)
    h1 = h_ref[...] + g1 * _dot(m, wo_ref[...])
    u2 = _rmsnorm_rows(h1, g2_ref[...]) * (1.0 + sc2) + sh2
    u_hi, u_lo = _split_bf16(u2)

    gu = _dot(u_hi, wsgu_ref[...])
    dsh = wsd_ref.shape[0]
    act = (jax.nn.silu(gu[:, :dsh]) * gu[:, dsh:]).astype(bf16)
    hmid_ref[...] = h1 + g2 * _dot(act, wsd_ref[...])

    for j in range(d // LANES):
        urow_ref[pl.ds(j, tm, stride=SUBLANES), :] = u2[:, j * LANES:(j + 1) * LANES]

    logits = _dot_nt(wrh_ref[...], u_hi) + _dot_nt(wrh_ref[...], u_lo) + _dot_nt(wrl_ref[...], u_hi)
    scores = jax.nn.sigmoid(logits)
    work = scores + rb_ref[...]
    n_e = scores.shape[0]
    e_iota = lax.broadcasted_iota(jnp.int32, (n_e, tm), 0).astype(f32)
    onehots, idxs, svals = [], [], []
    for _ in range(TOP_K):
        mx = jnp.max(work, axis=0, keepdims=True)
        ik = jnp.min(jnp.where(work == mx, e_iota, float(n_e)), axis=0, keepdims=True)
        oh = e_iota == ik
        onehots.append(oh)
        idxs.append(ik.astype(jnp.int32))
        svals.append(jnp.sum(jnp.where(oh, scores, 0.0), axis=0, keepdims=True))
        work = jnp.where(oh, -jnp.inf, work)
    ssum = svals[0]
    for sv in svals[1:]:
        ssum = ssum + sv
    idx_ref[...] = jnp.concatenate(idxs, axis=0)
    wts_ref[...] = jnp.concatenate([sv / ssum * ROUTED_SCALE for sv in svals], axis=0)

    member = jnp.where(onehots[0], 1.0, 0.0)
    for oh in onehots[1:]:
        member = member + jnp.where(oh, 1.0, 0.0)
    ri = lax.broadcasted_iota(jnp.int32, (tm, tm), 0)
    ci = lax.broadcasted_iota(jnp.int32, (tm, tm), 1)
    upper = jnp.where(ri <= ci, 1.0, 0.0).astype(bf16)
    incl = _dot(member.astype(bf16), upper)
    base = jnp.concatenate([cnt[...]] * (tm // LANES), axis=1)
    pos = base + incl - member
    ranks = [jnp.sum(jnp.where(oh, pos, 0.0), axis=0, keepdims=True) for oh in onehots]
    rank_ref[...] = jnp.concatenate(ranks, axis=0).astype(jnp.int32)
    new_cnt = cnt[...] + jnp.broadcast_to(incl[:, tm - 1:tm], cnt.shape)
    cnt[...] = new_cnt
    cnt_ref[...] = new_cnt


def _post(ya_in, ob, ga, gb, h, mod, wa, wb, wo, g2, wsgu, wsd, wr_hi, wr_lo, rb):
    bsz, s, d = h.shape
    tm = TM_POST
    t = bsz * s
    nsb = s // tm
    row = pl.BlockSpec((None, tm, d), lambda b, i: (b, i, 0))
    full = lambda a: pl.BlockSpec(a.shape, lambda b, i: (0,) * a.ndim)
    tok = pl.BlockSpec((TOP_K, tm), lambda b, i: (0, b * nsb + i))
    return pl.pallas_call(
        _post_kernel,
        grid=(bsz, nsb),
        in_specs=[row, row, row, row, row,
                  pl.BlockSpec((None, 1, mod.shape[-1]), lambda b, i: (b, 0, 0)),
                  full(wa), full(wb), full(wo), full(g2), full(wsgu), full(wsd),
                  full(wr_hi), full(wr_lo), full(rb)],
        out_specs=[row,
                   pl.BlockSpec((tm * SUBLANES, LANES), lambda b, i: (b * nsb + i, 0)),
                   tok, tok, tok,
                   pl.BlockSpec((N_EXPERTS, LANES), lambda b, i: (0, 0))],
        out_shape=[jax.ShapeDtypeStruct((bsz, s, d), f32),
                   jax.ShapeDtypeStruct((t * SUBLANES, LANES), f32),
                   jax.ShapeDtypeStruct((TOP_K, t), jnp.int32),
                   jax.ShapeDtypeStruct((TOP_K, t), f32),
                   jax.ShapeDtypeStruct((TOP_K, t), jnp.int32),
                   jax.ShapeDtypeStruct((N_EXPERTS, LANES), f32)],
        scratch_shapes=[pltpu.VMEM((N_EXPERTS, LANES), f32)],
        compiler_params=_cparams(2),
        name="post",
    )(ya_in, ob, ga, gb, h, mod, wa, wb, wo, g2, wsgu, wsd, wr_hi, wr_lo, rb)


def _dest_kernel(idx_ref, rank_ref, pstart_ref, o_ref):
    tn = idx_ref.shape[1]
    n_e = pstart_ref.shape[0]
    e_iota = lax.broadcasted_iota(jnp.int32, (n_e, tn), 0)
    rows = []
    for k in range(TOP_K):
        start = jnp.sum(jnp.where(e_iota == idx_ref[k:k + 1, :], pstart_ref[...], 0.0),
                        axis=0, keepdims=True)
        rows.append(start.astype(jnp.int32) + rank_ref[k:k + 1, :])
    o_ref[...] = jnp.concatenate(rows, axis=0)


def _dest(idx_t, rank_t, pstart_col):
    t = idx_t.shape[1]
    tn = TN_DEST
    tok = pl.BlockSpec((TOP_K, tn), lambda i: (0, i))
    return pl.pallas_call(
        _dest_kernel,
        grid=(t // tn,),
        in_specs=[tok, tok, pl.BlockSpec(pstart_col.shape, lambda i: (0, 0))],
        out_specs=tok,
        out_shape=jax.ShapeDtypeStruct((TOP_K, t), jnp.int32),
        compiler_params=_cparams(1),
        name="dest",
    )(idx_t, rank_t, pstart_col)


def _disp_kernel(dest_ref, u_ref, xg_in_ref, xg_ref, sem):
    del xg_in_ref
    td = u_ref.shape[0]

    def issue(t, carry):
        for k in range(TOP_K):
            pltpu.make_async_copy(u_ref.at[t], xg_ref.at[dest_ref[k, t]], sem).start()
        return carry

    lax.fori_loop(0, td, issue, 0)
    for _ in range(TOP_K):
        pltpu.make_async_copy(u_ref, xg_ref.at[pl.ds(0, td)], sem).wait()


def _dispatch(dest_blk, urows, xg_zero):
    t = urows.shape[0]
    td = TD_DISP
    return pl.pallas_call(
        _disp_kernel,
        grid=(t // td,),
        in_specs=[
            pl.BlockSpec((None, TOP_K, td), lambda i: (i, 0, 0), memory_space=pltpu.SMEM),
            pl.BlockSpec((td, SUBLANES, LANES), lambda i: (i, 0, 0)),
            pl.BlockSpec(memory_space=pl.ANY),
        ],
        out_specs=pl.BlockSpec(memory_space=pl.ANY),
        out_shape=jax.ShapeDtypeStruct(xg_zero.shape, f32),
        scratch_shapes=[pltpu.SemaphoreType.DMA],
        input_output_aliases={2: 0},
        compiler_params=_cparams(1),
        name="dispatch",
    )(dest_blk, urows, xg_zero)


def _experts_kernel(be_ref, nused_ref, x_ref, wgu_ref, wd_ref, y_ref):
    del be_ref
    rows = x_ref.shape[0] // SUBLANES
    d = wgu_ref.shape[0]
    i = pl.program_id(0)

    @pl.when(i < nused_ref[0])
    def _():
        x = jnp.concatenate(
            [x_ref[pl.ds(j, rows, stride=SUBLANES), :] for j in range(d // LANES)], axis=1
        ).astype(bf16)
        gu = _dot(x, wgu_ref[...])
        de = wd_ref.shape[0]
        act = (jax.nn.silu(gu[:, :de]) * gu[:, de:]).astype(bf16)
        y = _dot(act, wd_ref[...])
        for j in range(d // LANES):
            y_ref[pl.ds(j, rows, stride=SUBLANES), :] = y[:, j * LANES:(j + 1) * LANES]

    @pl.when(i >= nused_ref[0])
    def _():
        y_ref[...] = jnp.zeros_like(y_ref)


def _experts(block_exp, nused, xg2, w_gu, w_down):
    p8 = xg2.shape[0]
    n_e, d, de2 = w_gu.shape
    rb = R_EXP * SUBLANES
    nb = p8 // rb
    grid_spec = pltpu.PrefetchScalarGridSpec(
        num_scalar_prefetch=2,
        grid=(nb,),
        in_specs=[
            pl.BlockSpec((rb, LANES), lambda i, be, nu: (i, 0)),
            pl.BlockSpec((None, d, de2), lambda i, be, nu: (be[i], 0, 0)),
            pl.BlockSpec((None, de2 // 2, d), lambda i, be, nu: (be[i], 0, 0)),
        ],
        out_specs=pl.BlockSpec((rb, LANES), lambda i, be, nu: (i, 0)),
    )
    return pl.pallas_call(
        _experts_kernel,
        grid_spec=grid_spec,
        out_shape=jax.ShapeDtypeStruct((p8, LANES), f32),
        compiler_params=_cparams(1),
        name="experts",
    )(block_exp, nused, xg2, w_gu, w_down)


def _comb_kernel(dest_ref, wts_ref, hmid_ref, mod_ref, fg_ref, yg_ref, o_ref, buf, acc, sem,
                 *, final):
    tc, d = hmid_ref.shape

    def issue(t, carry):
        for k in range(TOP_K):
            pltpu.make_async_copy(yg_ref.at[dest_ref[k, t]], buf.at[k * tc + t], sem).start()
        return carry

    lax.fori_loop(0, tc, issue, 0)
    for k in range(TOP_K):
        pltpu.make_async_copy(yg_ref.at[pl.ds(0, tc)], buf.at[pl.ds(k * tc, tc)], sem).wait()

    def reduce(t, carry):
        a = buf[t] * wts_ref[0, t]
        for k in range(1, TOP_K):
            a = a + buf[k * tc + t] * wts_ref[k, t]
        acc[pl.ds(pl.multiple_of(t * SUBLANES, SUBLANES), SUBLANES), :] = a
        return carry

    lax.fori_loop(0, tc, reduce, 0)
    routed = jnp.concatenate(
        [acc[pl.ds(j, tc, stride=SUBLANES), :] for j in range(d // LANES)], axis=1)
    g2 = mod_ref[:, 5 * d:6 * d]
    h_new = hmid_ref[...] + g2 * routed
    if final:
        h_new = _rmsnorm_rows(h_new, fg_ref[...])
    o_ref[...] = h_new


def _combine(dest_blk, wts_blk, hmid, mod, final_g, yg, final):
    bsz, s, d = hmid.shape
    tc = TC_COMB
    nsb = s // tc
    row = pl.BlockSpec((None, tc, d), lambda b, i: (b, i, 0))
    smem_tok = pl.BlockSpec((None, TOP_K, tc), lambda b, i: (b * nsb + i, 0, 0),
                            memory_space=pltpu.SMEM)
    return pl.pallas_call(
        functools.partial(_comb_kernel, final=final),
        grid=(bsz, nsb),
        in_specs=[smem_tok, smem_tok, row,
                  pl.BlockSpec((None, 1, mod.shape[-1]), lambda b, i: (b, 0, 0)),
                  pl.BlockSpec(final_g.shape, lambda b, i: (0, 0)),
                  pl.BlockSpec(memory_space=pl.ANY)],
        out_specs=row,
        out_shape=jax.ShapeDtypeStruct((bsz, s, d), f32),
        scratch_shapes=[pltpu.VMEM((TOP_K * tc, SUBLANES, LANES), f32),
                        pltpu.VMEM((tc * SUBLANES, LANES), f32),
                        pltpu.SemaphoreType.DMA],
        compiler_params=_cparams(2),
        name="combine_final" if final else "combine",
    )(dest_blk, wts_blk, hmid, mod, final_g, yg)


def _tok_blocks(a_t, blk):
    k, t = a_t.shape
    return a_t.reshape(k, t // blk, blk).transpose(1, 0, 2)


def kernel(x, c, ada_w, ada_b, norm1_g, w_in, conv_w, conv_b, rg_wa, rg_ba, rg_wx, rg_bx, rg_lambda,
           gla_w_up, gla_b_up, gla_norm_g, w_branch_a, w_branch_b, w_out, norm2_g, router_w, router_b,
           exp_w_gu, exp_w_down, shared_w_gu, shared_w_down, final_g):
    bsz, s, d = x.shape
    n_layers = ada_w.shape[0]
    t = bsz * s
    dqk = GLA_HEADS * GLA_DK
    n_assign = t * TOP_K
    nb = -(-(n_assign + N_EXPERTS * (R_EXP - 1)) // R_EXP)
    p = nb * R_EXP

    mod_all = _ada(c, ada_w, ada_b)
    h = x
    for l in range(n_layers):
        mod = mod_all[l].reshape(bsz, 1, 6 * d)
        wl = w_in[l]
        lr0 = 2 * d + 2 * dqk + 2 * d
        w_main = jnp.concatenate([wl[:, :lr0], wl[:, lr0 + GLA_RANK:]], axis=1).astype(bf16)
        w_lr = jnp.pad(wl[:, lr0:lr0 + GLA_RANK], ((0, 0), (0, LANES - GLA_RANK))).astype(bf16)
        w_up = jnp.pad(gla_w_up[l], ((0, LANES - GLA_RANK), (0, 0))).astype(bf16)
        xr, gg, q, k, v, og, ga, gb, la = _in_proj(
            h, mod, norm1_g[l].reshape(1, d), w_main, w_lr, w_up, gla_b_up[l].reshape(1, dqk))

        wab = jnp.concatenate([rg_wa[l], rg_wx[l]], axis=-1).astype(bf16)
        ya_in = _rglru(xr, gg, conv_w[l], conv_b[l].reshape(1, d), wab,
                       rg_ba[l].reshape(1, d), rg_bx[l].reshape(1, d), rg_lambda[l].reshape(1, d))
        ob = _gla(q, k, v, la, og, gla_norm_g[l].reshape(1, d))

        wr_t = router_w[l].T
        wr_hi = wr_t.astype(bf16)
        wr_lo = (wr_t - wr_hi.astype(f32)).astype(bf16)
        hmid, urows, idx_t, wts_t, rank_t, counts = _post(
            ya_in, ob, ga, gb, h, mod,
            w_branch_a[l].astype(bf16), w_branch_b[l].astype(bf16), w_out[l].astype(bf16),
            norm2_g[l].reshape(1, d), shared_w_gu[l].astype(bf16), shared_w_down[l].astype(bf16),
            wr_hi, wr_lo, router_b[l].reshape(N_EXPERTS, 1))

        cnt = counts[:, 0].astype(jnp.int32)
        padded = (cnt + R_EXP - 1) // R_EXP * R_EXP
        ends = jnp.cumsum(padded)
        pstart = ends - padded
        block_exp = jnp.minimum(
            jnp.searchsorted(ends, jnp.arange(nb, dtype=jnp.int32) * R_EXP, side="right"),
            N_EXPERTS - 1).astype(jnp.int32)
        nused = (ends[-1:] // R_EXP).astype(jnp.int32)

        dest_t = _dest(idx_t, rank_t, pstart.astype(f32).reshape(N_EXPERTS, 1))
        xg = _dispatch(_tok_blocks(dest_t, TD_DISP), urows.reshape(t, SUBLANES, LANES),
                       jnp.zeros((p, SUBLANES, LANES), f32))
        yg = _experts(block_exp, nused, xg.reshape(p * SUBLANES, LANES),
                      exp_w_gu[l].astype(bf16), exp_w_down[l].astype(bf16))
        h = _combine(_tok_blocks(dest_t, TC_COMB), _tok_blocks(wts_t, TC_COMB), hmid, mod,
                     final_g.reshape(1, d), yg.reshape(p, SUBLANES, LANES), l == n_layers - 1)
    return h
```

```python
import functools

import jax
import jax.numpy as jnp
import numpy as np
from jax import lax
from jax.experimental import pallas as pl
from jax.experimental.pallas import tpu as pltpu

EPS = 1e-6
CHUNK = 64
CONV_W = 4
RG_C = 8.0
RNN_BLOCKS = 8
GLA_HEADS = 4
GLA_DK = 128
GLA_DV = 256
GLA_RANK = 16
GLA_TAU = 16.0
N_EXPERTS = 128
TOP_K = 8
D_EXPERT = 256
ROUTED_SCALE = 2.5

LANES = 128
SUBLANES = 8
VMEM_LIMIT = 56 * 1024 * 1024

TM_IN = 256
TS_RNN = 256
TS_GLA = 256
TM_POST = 256
G_TOK = 512
R_EXP = 256

bf16 = jnp.bfloat16
f32 = jnp.float32


def _cparams(n_axes):
    return pltpu.CompilerParams(
        dimension_semantics=("arbitrary",) * n_axes,
        vmem_limit_bytes=VMEM_LIMIT,
    )


def _dot(a, b):
    return jnp.dot(a, b, preferred_element_type=f32)


def _dot_nt(a, b):
    return lax.dot_general(a, b, (((1,), (1,)), ((), ())), preferred_element_type=f32)


def _dot_tn(a, b):
    return lax.dot_general(a, b, (((0,), (0,)), ((), ())), preferred_element_type=f32)


def _split_bf16(x):
    hi = x.astype(bf16)
    lo = (x - hi.astype(f32)).astype(bf16)
    return hi, lo


def _rmsnorm_rows(x, g):
    return x * lax.rsqrt(jnp.mean(x * x, axis=-1, keepdims=True) + EPS) * g


def _log_sigmoid(x):
    return jnp.minimum(x, 0.0) - jnp.log1p(jnp.exp(-jnp.abs(x)))


def _softplus(x):
    return jnp.maximum(x, 0.0) + jnp.log1p(jnp.exp(-jnp.abs(x)))


def _ada_kernel(c_ref, w_ref, b_ref, o_ref):
    c_act = jax.nn.silu(c_ref[...])
    c_hi, c_lo = _split_bf16(c_act)
    w_hi, w_lo = _split_bf16(w_ref[...])
    acc = _dot(c_hi, w_hi) + _dot(c_lo, w_hi) + _dot(c_hi, w_lo)
    o_ref[...] = acc + b_ref[...]


def _ada(c, ada_w, ada_b):
    n_layers, d, n6 = ada_w.shape
    bsz = c.shape[0]
    tn = 512
    return pl.pallas_call(
        _ada_kernel,
        grid=(n_layers, n6 // tn),
        in_specs=[
            pl.BlockSpec((bsz, d), lambda l, j: (0, 0)),
            pl.BlockSpec((None, d, tn), lambda l, j: (l, 0, j)),
            pl.BlockSpec((None, 1, tn), lambda l, j: (l, 0, j)),
        ],
        out_specs=pl.BlockSpec((None, bsz, tn), lambda l, j: (l, 0, j)),
        out_shape=jax.ShapeDtypeStruct((n_layers, bsz, n6), f32),
        compiler_params=_cparams(2),
        name="ada_mod",
    )(c, ada_w, ada_b.reshape(n_layers, 1, n6))


def _in_kernel(h_ref, mod_ref, g_ref, w_ref, wlr_ref, wup_ref, bup_ref,
               xr_ref, gg_ref, q_ref, k_ref, v_ref, og_ref, ga_ref, gb_ref, la_ref, u_scr):
    d = h_ref.shape[-1]
    sh1 = mod_ref[:, 0:d]
    sc1 = mod_ref[:, d:2 * d]
    u = _rmsnorm_rows(h_ref[...], g_ref[...]) * (1.0 + sc1) + sh1
    u_scr[...] = u.astype(bf16)

    def proj(c0, width):
        return _dot(u_scr[...], w_ref[:, c0:c0 + width])

    cw = 512
    for c in range(d // cw):
        xr_ref[:, c * cw:(c + 1) * cw] = proj(c * cw, cw)
    for c in range(d // cw):
        y = proj(d + c * cw, cw)
        gg_ref[:, c * cw:(c + 1) * cw] = jax.nn.gelu(y, approximate=True).astype(bf16)
    dqk = GLA_HEADS * GLA_DK
    q_ref[...] = (proj(2 * d, dqk) * (GLA_DK ** -0.5)).astype(bf16)
    k_ref[...] = proj(2 * d + dqk, dqk).astype(bf16)
    base = 2 * d + 2 * dqk
    for c in range(d // cw):
        v_ref[:, c * cw:(c + 1) * cw] = proj(base + c * cw, cw).astype(bf16)
    base += d
    for c in range(d // cw):
        y = proj(base + c * cw, cw)
        og_ref[:, c * cw:(c + 1) * cw] = jax.nn.silu(y).astype(bf16)
    base += d
    for c in range(d // cw):
        y = proj(base + c * cw, cw)
        ga_ref[:, c * cw:(c + 1) * cw] = jax.nn.sigmoid(y).astype(bf16)
    base += d
    for c in range(d // cw):
        y = proj(base + c * cw, cw)
        gb_ref[:, c * cw:(c + 1) * cw] = jax.nn.sigmoid(y).astype(bf16)
    lr = _dot(u_scr[...], wlr_ref[...]).astype(bf16)
    logit = _dot(lr, wup_ref[...]) + bup_ref[...]
    la_ref[...] = _log_sigmoid(logit) * (1.0 / GLA_TAU)


def _in_proj(h, mod, norm_g, w_main, w_lr, w_up, b_up):
    bsz, s, d = h.shape
    tm = TM_IN
    dqk = GLA_HEADS * GLA_DK
    row = lambda width: pl.BlockSpec((None, tm, width), lambda b, i: (b, i, 0))
    full = lambda a: pl.BlockSpec(a.shape, lambda b, i: (0,) * a.ndim)
    out_shapes = [
        jax.ShapeDtypeStruct((bsz, s, d), f32),
        jax.ShapeDtypeStruct((bsz, s, d), bf16),
        jax.ShapeDtypeStruct((bsz, s, dqk), bf16),
        jax.ShapeDtypeStruct((bsz, s, dqk), bf16),
        jax.ShapeDtypeStruct((bsz, s, d), bf16),
        jax.ShapeDtypeStruct((bsz, s, d), bf16),
        jax.ShapeDtypeStruct((bsz, s, d), bf16),
        jax.ShapeDtypeStruct((bsz, s, d), bf16),
        jax.ShapeDtypeStruct((bsz, s, dqk), f32),
    ]
    return pl.pallas_call(
        _in_kernel,
        grid=(bsz, s // tm),
        in_specs=[
            row(d),
            pl.BlockSpec((None, 1, mod.shape[-1]), lambda b, i: (b, 0, 0)),
            full(norm_g),
            pl.BlockSpec(w_main.shape, lambda b, i: (0, 0), pipeline_mode=pl.Buffered(1)),
            full(w_lr), full(w_up), full(b_up),
        ],
        out_specs=[row(d), row(d), row(dqk), row(dqk), row(d), row(d), row(d), row(d), row(dqk)],
        out_shape=out_shapes,
        scratch_shapes=[pltpu.VMEM((tm, d), bf16)],
        compiler_params=_cparams(2),
        name="in_proj",
    )(h, mod, norm_g, w_main, w_lr, w_up, b_up)


def _rglru_kernel(xr_ref, gg_ref, cw_ref, cb_ref, wab_ref, ba_ref, bx_ref, lam_ref,
                  o_ref, xpad, hcar):
    ts, d = xr_ref.shape
    si = pl.program_id(1)

    @pl.when(si == 0)
    def _():
        xpad[0:SUBLANES, :] = jnp.zeros((SUBLANES, d), f32)
        hcar[...] = jnp.zeros_like(hcar)

    xpad[SUBLANES:SUBLANES + ts, :] = xr_ref[...]
    xc = cb_ref[...] + xpad[SUBLANES:SUBLANES + ts, :] * cw_ref[CONV_W - 1:CONV_W, :]
    for j in range(CONV_W - 1):
        off = SUBLANES - (CONV_W - 1) + j
        xc = xc + xpad[off:off + ts, :] * cw_ref[j:j + 1, :]
    xpad[0:SUBLANES, :] = xpad[ts:ts + SUBLANES, :]

    groups = ts // SUBLANES
    bw = d // RNN_BLOCKS
    sub = lax.broadcasted_iota(jnp.int32, (groups, SUBLANES, bw), 1)
    first_row = (lax.broadcasted_iota(jnp.int32, (ts, bw), 0) == 0) & (si == 0)
    sp = _softplus(-lam_ref[...])
    xcb = xc.astype(bf16)
    for hb in range(RNN_BLOCKS):
        cs = slice(hb * bw, (hb + 1) * bw)
        x_h = xc[:, cs]
        g = _dot(xcb[:, cs], wab_ref[hb])
        r = jax.nn.sigmoid(g[:, :bw] + ba_ref[:, cs])
        i = jax.nn.sigmoid(g[:, bw:] + bx_ref[:, cs])
        log_a = (-RG_C) * r * sp[:, cs]
        a = jnp.exp(log_a)
        mult = jnp.sqrt(1.0 - a * a)
        mult = jnp.where(first_row, 1.0, mult)
        bt = mult * (i * x_h)
        av = a.reshape(groups, SUBLANES, bw)
        bv = bt.reshape(groups, SUBLANES, bw)
        for sh in (1, 2, 4):
            a_prev = pltpu.roll(av, sh, 1)
            b_prev = pltpu.roll(bv, sh, 1)
            keep = sub >= sh
            bv = jnp.where(keep, av * b_prev + bv, bv)
            av = jnp.where(keep, av * a_prev, av)
        hin = hcar[0:1, cs]
        outs = []
        for gi in range(groups):
            hg = av[gi] * hin + bv[gi]
            outs.append(hg)
            hin = hg[SUBLANES - 1:SUBLANES, :]
        hcar[0:1, cs] = hin
        hs = jnp.concatenate(outs, axis=0)
        o_ref[:, cs] = (hs * gg_ref[:, cs].astype(f32)).astype(bf16)


def _rglru(xr, gg, conv_w, conv_b, wab, ba, bx, lam):
    bsz, s, d = xr.shape
    ts = TS_RNN
    row = pl.BlockSpec((None, ts, d), lambda b, i: (b, i, 0))
    full = lambda a: pl.BlockSpec(a.shape, lambda b, i: (0,) * a.ndim)
    return pl.pallas_call(
        _rglru_kernel,
        grid=(bsz, s // ts),
        in_specs=[row, row, full(conv_w), full(conv_b), full(wab), full(ba), full(bx), full(lam)],
        out_specs=row,
        out_shape=jax.ShapeDtypeStruct((bsz, s, d), bf16),
        scratch_shapes=[pltpu.VMEM((ts + SUBLANES, d), f32), pltpu.VMEM((SUBLANES, d), f32)],
        compiler_params=_cparams(2),
        name="rglru",
    )(xr, gg, conv_w, conv_b, wab, ba, bx, lam)


def _gla_kernel(q_ref, k_ref, v_ref, la_ref, og_ref, g_ref, o_ref, st):
    ts = q_ref.shape[0]
    si = pl.program_id(1)

    @pl.when(si == 0)
    def _():
        st[...] = jnp.zeros_like(st)

    ri = lax.broadcasted_iota(jnp.int32, (CHUNK, CHUNK), 0)
    ci = lax.broadcasted_iota(jnp.int32, (CHUNK, CHUNK), 1)
    tri = jnp.where(ci <= ri, 1.0, 0.0).astype(bf16)
    for c in range(ts // CHUNK):
        rs = slice(c * CHUNK, (c + 1) * CHUNK)
        la_hi, la_lo = _split_bf16(la_ref[rs, :])
        cum = _dot(tri, la_hi) + _dot(tri, la_lo)
        tot = cum[CHUNK - 1:CHUNK, :]
        kdec = (k_ref[rs, :].astype(f32) * jnp.exp(tot - cum)).astype(bf16)
        dec = jnp.exp(tot)
        for hd in range(GLA_HEADS):
            ks = slice(hd * GLA_DK, (hd + 1) * GLA_DK)
            vs = slice(hd * GLA_DV, (hd + 1) * GLA_DV)
            s_new = st[hd] * dec[:, ks] + _dot_tn(v_ref[rs, vs], kdec[:, ks])
            st[hd] = s_new
            o = _dot_nt(q_ref[rs, ks], s_new.astype(bf16))
            o = _rmsnorm_rows(o, g_ref[:, vs])
            o_ref[rs, vs] = (o * og_ref[rs, vs].astype(f32)).astype(bf16)


def _gla(q, k, v, la, og, norm_g):
    bsz, s, dqk = q.shape
    d = v.shape[-1]
    ts = TS_GLA
    row = lambda width: pl.BlockSpec((None, ts, width), lambda b, i: (b, i, 0))
    return pl.pallas_call(
        _gla_kernel,
        grid=(bsz, s // ts),
        in_specs=[row(dqk), row(dqk), row(d), row(dqk), row(d),
                  pl.BlockSpec(norm_g.shape, lambda b, i: (0, 0))],
        out_specs=row(d),
        out_shape=jax.ShapeDtypeStruct((bsz, s, d), bf16),
        scratch_shapes=[pltpu.VMEM((GLA_HEADS, GLA_DV, GLA_DK), f32)],
        compiler_params=_cparams(2),
        name="gla",
    )(q, k, v, la, og, norm_g)


def _post_kernel(ya_ref, ob_ref, ga_ref, gb_ref, h_ref, mod_ref, wa_ref, wb_ref, wo_ref, g2_ref,
                 wsgu_ref, wsd_ref, wrh_ref, wrl_ref, rb_ref,
                 hmid_ref, urow_ref, idx_ref, wts_ref, rank_ref, cnt_ref, cnt):
    tm, d = h_ref.shape
    group_start = pl.program_id(1) % (G_TOK // tm) == 0

    @pl.when(group_start)
    def _():
        cnt[...] = jnp.zeros_like(cnt)

    g1 = mod_ref[:, 2 * d:3 * d]
    sh2 = mod_ref[:, 3 * d:4 * d]
    sc2 = mod_ref[:, 4 * d:5 * d]
    g2 = mod_ref[:, 5 * d:6 * d]

    ya = _dot(ya_ref[...], wa_ref[...])
    yb = _dot(ob_ref[...], wb_ref[...])
    m = (ga_ref[...].astype(f32) * ya + gb_ref[...].astype(f32) * yb).astype(bf16)
    h1 = h_ref[...] + g1 * _dot(m, wo_ref[...])
    u2 = _rmsnorm_rows(h1, g2_ref[...]) * (1.0 + sc2) + sh2
    u_hi, u_lo = _split_bf16(u2)

    gu = _dot(u_hi, wsgu_ref[...])
    dsh = wsd_ref.shape[0]
    act = (jax.nn.silu(gu[:, :dsh]) * gu[:, dsh:]).astype(bf16)
    hmid_ref[...] = h1 + g2 * _dot(act, wsd_ref[...])

    for j in range(d // LANES):
        urow_ref[pl.ds(j, tm, stride=SUBLANES), :] = u2[:, j * LANES:(j + 1) * LANES]

    logits = _dot_nt(wrh_ref[...], u_hi) + _dot_nt(wrh_ref[...], u_lo) + _dot_nt(wrl_ref[...], u_hi)
    scores = jax.nn.sigmoid(logits)
    work = scores + rb_ref[...]
    n_e = scores.shape[0]
    e_iota = lax.broadcasted_iota(jnp.int32, (n_e, tm), 0).astype(f32)
    onehots, idxs, svals = [], [], []
    for _ in range(TOP_K):
        mx = jnp.max(work, axis=0, keepdims=True)
        ik = jnp.min(jnp.where(work == mx, e_iota, float(n_e)), axis=0, keepdims=True)
        oh = e_iota == ik
        onehots.append(oh)
        idxs.append(ik.astype(jnp.int32))
        svals.append(jnp.sum(jnp.where(oh, scores, 0.0), axis=0, keepdims=True))
        work = jnp.where(oh, -jnp.inf, work)
    ssum = svals[0]
    for sv in svals[1:]:
        ssum = ssum + sv
    idx_ref[...] = jnp.concatenate(idxs, axis=0)
    wts_ref[...] = jnp.concatenate([sv / ssum * ROUTED_SCALE for sv in svals], axis=0)

    member = jnp.where(onehots[0], 1.0, 0.0)
    for oh in onehots[1:]:
        member = member + jnp.where(oh, 1.0, 0.0)
    ri = lax.broadcasted_iota(jnp.int32, (tm, tm), 0)
    ci = lax.broadcasted_iota(jnp.int32, (tm, tm), 1)
    upper = jnp.where(ri <= ci, 1.0, 0.0).astype(bf16)
    incl = _dot(member.astype(bf16), upper)
    base = jnp.concatenate([cnt[...]] * (tm // LANES), axis=1)
    pos = base + incl - member
    ranks = [jnp.sum(jnp.where(oh, pos, 0.0), axis=0, keepdims=True) for oh in onehots]
    rank_ref[...] = jnp.concatenate(ranks, axis=0).astype(jnp.int32)
    new_cnt = cnt[...] + jnp.broadcast_to(incl[:, tm - 1:tm], cnt.shape)
    cnt[...] = new_cnt
    cnt_ref[...] = new_cnt


def _post(ya_in, ob, ga, gb, h, mod, wa, wb, wo, g2, wsgu, wsd, wr_hi, wr_lo, rb):
    bsz, s, d = h.shape
    tm = TM_POST
    t = bsz * s
    nsb = s // tm
    tpg = G_TOK // tm
    n_groups = t // G_TOK
    row = pl.BlockSpec((None, tm, d), lambda b, i: (b, i, 0))
    full = lambda a: pl.BlockSpec(a.shape, lambda b, i: (0,) * a.ndim)
    tok = pl.BlockSpec((TOP_K, tm), lambda b, i: (0, b * nsb + i))
    return pl.pallas_call(
        _post_kernel,
        grid=(bsz, nsb),
        in_specs=[row, row, row, row, row,
                  pl.BlockSpec((None, 1, mod.shape[-1]), lambda b, i: (b, 0, 0)),
                  full(wa), full(wb), full(wo), full(g2), full(wsgu), full(wsd),
                  full(wr_hi), full(wr_lo), full(rb)],
        out_specs=[row,
                   pl.BlockSpec((tm * SUBLANES, LANES), lambda b, i: (b * nsb + i, 0)),
                   tok, tok, tok,
                   pl.BlockSpec((None, N_EXPERTS, LANES),
                                lambda b, i: ((b * nsb + i) // tpg, 0, 0))],
        out_shape=[jax.ShapeDtypeStruct((bsz, s, d), f32),
                   jax.ShapeDtypeStruct((t * SUBLANES, LANES), f32),
                   jax.ShapeDtypeStruct((TOP_K, t), jnp.int32),
                   jax.ShapeDtypeStruct((TOP_K, t), f32),
                   jax.ShapeDtypeStruct((TOP_K, t), jnp.int32),
                   jax.ShapeDtypeStruct((n_groups, N_EXPERTS, LANES), f32)],
        scratch_shapes=[pltpu.VMEM((N_EXPERTS, LANES), f32)],
        compiler_params=_cparams(2),
        name="post",
    )(ya_in, ob, ga, gb, h, mod, wa, wb, wo, g2, wsgu, wsd, wr_hi, wr_lo, rb)


def _dest_kernel(idx_ref, rank_ref, pstart_ref, o_ref):
    tn = idx_ref.shape[1]
    n_e = pstart_ref.shape[0]
    e_iota = lax.broadcasted_iota(jnp.int32, (n_e, tn), 0)
    rows = []
    for k in range(TOP_K):
        start = jnp.sum(jnp.where(e_iota == idx_ref[k:k + 1, :], pstart_ref[...], 0.0),
                        axis=0, keepdims=True)
        rows.append(start.astype(jnp.int32) + rank_ref[k:k + 1, :])
    o_ref[...] = jnp.concatenate(rows, axis=0)


def _dest(idx_t, rank_t, lstart_col):
    t = idx_t.shape[1]
    tn = G_TOK
    tok = pl.BlockSpec((TOP_K, tn), lambda i: (0, i))
    return pl.pallas_call(
        _dest_kernel,
        grid=(t // tn,),
        in_specs=[tok, tok, pl.BlockSpec((None, N_EXPERTS, 1), lambda i: (i, 0, 0))],
        out_specs=tok,
        out_shape=jax.ShapeDtypeStruct((TOP_K, t), jnp.int32),
        compiler_params=_cparams(1),
        name="dest",
    )(idx_t, rank_t, lstart_col)


TOK_UNROLL = 4


def _disp_kernel(dest_ref, cnt_ref, gst_ref, pads_ref, padn_ref, nused_ref, u_ref, xg_ref,
                 xloc, zbuf, sem):
    g_tok = u_ref.shape[0]
    n_rows = g_tok * TOP_K
    g = pl.program_id(0)
    last = pl.num_programs(0) - 1
    slot = g % 2

    def drain(s):
        pltpu.make_async_copy(xloc.at[s], xg_ref.at[pl.ds(0, n_rows)], sem.at[s]).wait()

    @pl.when(g == 0)
    def _():
        zbuf[...] = jnp.zeros_like(zbuf)

    @pl.when(g >= 2)
    def _():
        drain(slot)

    xs = xloc.at[slot]

    def fill(i, carry):
        for uu in range(TOK_UNROLL):
            t = i * TOK_UNROLL + uu
            row = u_ref[t]
            for k in range(TOP_K):
                xs[dest_ref[k, t]] = row
        return carry

    lax.fori_loop(0, g_tok // TOK_UNROLL, fill, 0)

    def seg(e, ls):
        c = cnt_ref[0, e]

        @pl.when(c > 0)
        def _():
            pltpu.make_async_copy(xs.at[pl.ds(ls, c)], xg_ref.at[pl.ds(gst_ref[0, e], c)],
                                  sem.at[slot]).start()

        return ls + c

    lax.fori_loop(0, N_EXPERTS, seg, 0)

    @pl.when(g == last)
    def _():
        def pad_copy(e):
            n = padn_ref[0, e]
            return pltpu.make_async_copy(zbuf.at[pl.ds(0, n)], xg_ref.at[pl.ds(pads_ref[0, e], n)],
                                         sem.at[2])

        def pad_start(e, carry):
            @pl.when(padn_ref[0, e] > 0)
            def _():
                pad_copy(e).start()
            return carry

        def pad_wait(e, carry):
            @pl.when(padn_ref[0, e] > 0)
            def _():
                pad_copy(e).wait()
            return carry

        n_blocks = xg_ref.shape[0] // R_EXP

        def tail_copy(i):
            return pltpu.make_async_copy(
                zbuf, xg_ref.at[pl.ds(pl.multiple_of(i * R_EXP, R_EXP), R_EXP)], sem.at[2])

        def tail_start(i, carry):
            tail_copy(i).start()
            return carry

        def tail_wait(i, carry):
            tail_copy(i).wait()
            return carry

        lax.fori_loop(0, N_EXPERTS, pad_start, 0)
        lax.fori_loop(nused_ref[0, 0], n_blocks, tail_start, 0)
        lax.fori_loop(0, N_EXPERTS, pad_wait, 0)
        lax.fori_loop(nused_ref[0, 0], n_blocks, tail_wait, 0)
        drain(1 - slot)
        drain(slot)


def _dispatch(dest_blk, cnt_tab, gst_tab, pad_start, pad_len, nused, urows, p_rows):
    t = urows.shape[0]
    n_groups = t // G_TOK
    assert n_groups >= 2
    smem_grp = lambda: pl.BlockSpec((None, 1, N_EXPERTS), lambda i: (i, 0, 0),
                                    memory_space=pltpu.SMEM)
    smem_all = lambda: pl.BlockSpec((1, N_EXPERTS), lambda i: (0, 0), memory_space=pltpu.SMEM)
    return pl.pallas_call(
        _disp_kernel,
        grid=(n_groups,),
        in_specs=[
            pl.BlockSpec((None, TOP_K, G_TOK), lambda i: (i, 0, 0), memory_space=pltpu.SMEM),
            smem_grp(), smem_grp(), smem_all(), smem_all(),
            pl.BlockSpec((1, 1), lambda i: (0, 0), memory_space=pltpu.SMEM),
            pl.BlockSpec((G_TOK, SUBLANES, LANES), lambda i: (i, 0, 0)),
        ],
        out_specs=pl.BlockSpec(memory_space=pl.ANY),
        out_shape=jax.ShapeDtypeStruct((p_rows, SUBLANES, LANES), f32),
        scratch_shapes=[pltpu.VMEM((2, G_TOK * TOP_K, SUBLANES, LANES), f32),
                        pltpu.VMEM((R_EXP, SUBLANES, LANES), f32),
                        pltpu.SemaphoreType.DMA((3,))],
        compiler_params=_cparams(1),
        name="dispatch",
    )(dest_blk, cnt_tab, gst_tab, pad_start, pad_len, nused.reshape(1, 1), urows)


def _experts_kernel(be_ref, nused_ref, x_ref, wgu_ref, wd_ref, y_ref):
    del be_ref
    rows = x_ref.shape[0] // SUBLANES
    d = wgu_ref.shape[0]
    i = pl.program_id(0)

    @pl.when(i < nused_ref[0])
    def _():
        x = jnp.concatenate(
            [x_ref[pl.ds(j, rows, stride=SUBLANES), :] for j in range(d // LANES)], axis=1
        ).astype(bf16)
        gu = _dot(x, wgu_ref[...])
        de = wd_ref.shape[0]
        act = (jax.nn.silu(gu[:, :de]) * gu[:, de:]).astype(bf16)
        y = _dot(act, wd_ref[...])
        for j in range(d // LANES):
            y_ref[pl.ds(j, rows, stride=SUBLANES), :] = y[:, j * LANES:(j + 1) * LANES]

    @pl.when(i >= nused_ref[0])
    def _():
        y_ref[...] = jnp.zeros_like(y_ref)


def _experts(block_exp, nused, xg2, w_gu, w_down):
    p8 = xg2.shape[0]
    n_e, d, de2 = w_gu.shape
    rb = R_EXP * SUBLANES
    nb = p8 // rb
    grid_spec = pltpu.PrefetchScalarGridSpec(
        num_scalar_prefetch=2,
        grid=(nb,),
        in_specs=[
            pl.BlockSpec((rb, LANES), lambda i, be, nu: (jnp.minimum(i, nu[0] - 1), 0)),
            pl.BlockSpec((None, d, de2), lambda i, be, nu: (be[i], 0, 0)),
            pl.BlockSpec((None, de2 // 2, d), lambda i, be, nu: (be[i], 0, 0)),
        ],
        out_specs=pl.BlockSpec((rb, LANES), lambda i, be, nu: (i, 0)),
    )
    return pl.pallas_call(
        _experts_kernel,
        grid_spec=grid_spec,
        out_shape=jax.ShapeDtypeStruct((p8, LANES), f32),
        compiler_params=_cparams(1),
        name="experts",
    )(block_exp, nused, xg2, w_gu, w_down)


def _comb_kernel(dest_ref, wts_ref, cnt_ref, gst_ref, cntn_ref, gstn_ref, hmid_ref, mod_ref, fg_ref,
                 yg_ref, o_ref, yloc, acc, sem, *, final):
    g_tok, d = hmid_ref.shape
    n_rows = g_tok * TOP_K
    g = pl.program_id(0)
    n_groups = pl.num_programs(0)
    slot = g % 2

    def fetch(cnt_r, gst_r, s):
        def seg(e, ls):
            c = cnt_r[0, e]

            @pl.when(c > 0)
            def _():
                pltpu.make_async_copy(yg_ref.at[pl.ds(gst_r[0, e], c)],
                                      yloc.at[s, pl.ds(ls, c)], sem.at[s]).start()

            return ls + c

        lax.fori_loop(0, N_EXPERTS, seg, 0)

    @pl.when(g == 0)
    def _():
        fetch(cnt_ref, gst_ref, 0)

    @pl.when(g + 1 < n_groups)
    def _():
        fetch(cntn_ref, gstn_ref, 1 - slot)

    pltpu.make_async_copy(yg_ref.at[pl.ds(0, n_rows)], yloc.at[slot], sem.at[slot]).wait()

    ys = yloc.at[slot]

    def reduce(i, carry):
        for uu in range(TOK_UNROLL):
            t = i * TOK_UNROLL + uu
            a = ys[dest_ref[0, t]] * wts_ref[0, t]
            for k in range(1, TOP_K):
                a = a + ys[dest_ref[k, t]] * wts_ref[k, t]
            acc[pl.ds(pl.multiple_of(t * SUBLANES, SUBLANES), SUBLANES), :] = a
        return carry

    lax.fori_loop(0, g_tok // TOK_UNROLL, reduce, 0)
    for j in range(d // LANES):
        cs = slice(j * LANES, (j + 1) * LANES)
        o_ref[:, cs] = hmid_ref[:, cs] + mod_ref[:, 5 * d + j * LANES:5 * d + (j + 1) * LANES] * \
            acc[pl.ds(j, g_tok, stride=SUBLANES), :]
    if final:
        o_ref[...] = _rmsnorm_rows(o_ref[...], fg_ref[...])


def _combine(dest_blk, wts_blk, cnt_tab, gst_tab, hmid, mod, final_g, yg, final):
    t, d = hmid.shape
    n_groups = t // G_TOK
    gpb = n_groups // mod.shape[0]
    row = pl.BlockSpec((G_TOK, d), lambda i: (i, 0))
    smem_tok = lambda: pl.BlockSpec((None, TOP_K, G_TOK), lambda i: (i, 0, 0),
                                    memory_space=pltpu.SMEM)
    smem_cur = lambda: pl.BlockSpec((None, 1, N_EXPERTS), lambda i: (i, 0, 0),
                                    memory_space=pltpu.SMEM)
    smem_nxt = lambda: pl.BlockSpec((None, 1, N_EXPERTS),
                                    lambda i: (jnp.minimum(i + 1, n_groups - 1), 0, 0),
                                    memory_space=pltpu.SMEM)
    return pl.pallas_call(
        functools.partial(_comb_kernel, final=final),
        grid=(n_groups,),
        in_specs=[smem_tok(), smem_tok(), smem_cur(), smem_cur(), smem_nxt(), smem_nxt(), row,
                  pl.BlockSpec((None, 1, mod.shape[-1]), lambda i: (i // gpb, 0, 0)),
                  pl.BlockSpec(final_g.shape, lambda i: (0, 0)),
                  pl.BlockSpec(memory_space=pl.ANY)],
        out_specs=row,
        out_shape=jax.ShapeDtypeStruct((t, d), f32),
        scratch_shapes=[pltpu.VMEM((2, G_TOK * TOP_K, SUBLANES, LANES), f32),
                        pltpu.VMEM((G_TOK * SUBLANES, LANES), f32),
                        pltpu.SemaphoreType.DMA((2,))],
        compiler_params=_cparams(1),
        name="combine_final" if final else "combine",
    )(dest_blk, wts_blk, cnt_tab, gst_tab, cnt_tab, gst_tab, hmid, mod, final_g, yg)


def _tok_blocks(a_t, blk):
    k, t = a_t.shape
    return a_t.reshape(k, t // blk, blk).transpose(1, 0, 2)


def kernel(x, c, ada_w, ada_b, norm1_g, w_in, conv_w, conv_b, rg_wa, rg_ba, rg_wx, rg_bx, rg_lambda,
           gla_w_up, gla_b_up, gla_norm_g, w_branch_a, w_branch_b, w_out, norm2_g, router_w, router_b,
           exp_w_gu, exp_w_down, shared_w_gu, shared_w_down, final_g):
    bsz, s, d = x.shape
    n_layers = ada_w.shape[0]
    t = bsz * s
    dqk = GLA_HEADS * GLA_DK
    n_assign = t * TOP_K
    nb = -(-(n_assign + N_EXPERTS * (R_EXP - 1)) // R_EXP)
    p = nb * R_EXP
    n_groups = t // G_TOK

    mod_all = _ada(c, ada_w, ada_b)
    h = x
    for l in range(n_layers):
        mod = mod_all[l].reshape(bsz, 1, 6 * d)
        wl = w_in[l]
        lr0 = 2 * d + 2 * dqk + 2 * d
        w_main = jnp.concatenate([wl[:, :lr0], wl[:, lr0 + GLA_RANK:]], axis=1).astype(bf16)
        w_lr = jnp.pad(wl[:, lr0:lr0 + GLA_RANK], ((0, 0), (0, LANES - GLA_RANK))).astype(bf16)
        w_up = jnp.pad(gla_w_up[l], ((0, LANES - GLA_RANK), (0, 0))).astype(bf16)
        xr, gg, q, k, v, og, ga, gb, la = _in_proj(
            h, mod, norm1_g[l].reshape(1, d), w_main, w_lr, w_up, gla_b_up[l].reshape(1, dqk))

        wab = jnp.concatenate([rg_wa[l], rg_wx[l]], axis=-1).astype(bf16)
        ya_in = _rglru(xr, gg, conv_w[l], conv_b[l].reshape(1, d), wab,
                       rg_ba[l].reshape(1, d), rg_bx[l].reshape(1, d), rg_lambda[l].reshape(1, d))
        ob = _gla(q, k, v, la, og, gla_norm_g[l].reshape(1, d))

        wr_t = router_w[l].T
        wr_hi = wr_t.astype(bf16)
        wr_lo = (wr_t - wr_hi.astype(f32)).astype(bf16)
        hmid, urows, idx_t, wts_t, rank_t, counts = _post(
            ya_in, ob, ga, gb, h, mod,
            w_branch_a[l].astype(bf16), w_branch_b[l].astype(bf16), w_out[l].astype(bf16),
            norm2_g[l].reshape(1, d), shared_w_gu[l].astype(bf16), shared_w_down[l].astype(bf16),
            wr_hi, wr_lo, router_b[l].reshape(N_EXPERTS, 1))

        cnt = counts[:, :, 0].astype(jnp.int32)
        lstart = jnp.cumsum(cnt, axis=1) - cnt
        ctot = jnp.sum(cnt, axis=0)
        padded = (ctot + R_EXP - 1) // R_EXP * R_EXP
        ends = jnp.cumsum(padded)
        pstart = ends - padded
        gstart = pstart[None, :] + jnp.cumsum(cnt, axis=0) - cnt
        block_exp = jnp.minimum(
            jnp.searchsorted(ends, jnp.arange(nb, dtype=jnp.int32) * R_EXP, side="right"),
            N_EXPERTS - 1).astype(jnp.int32)
        nused = (ends[-1:] // R_EXP).astype(jnp.int32)
        cnt_tab = cnt.reshape(n_groups, 1, N_EXPERTS)
        gst_tab = gstart.astype(jnp.int32).reshape(n_groups, 1, N_EXPERTS)

        dest_t = _dest(idx_t, rank_t, lstart.astype(f32).reshape(n_groups, N_EXPERTS, 1))
        dest_blk = _tok_blocks(dest_t, G_TOK)
        xg = _dispatch(dest_blk, cnt_tab, gst_tab,
                       (pstart + ctot).astype(jnp.int32).reshape(1, N_EXPERTS),
                       (padded - ctot).astype(jnp.int32).reshape(1, N_EXPERTS),
                       nused, urows.reshape(t, SUBLANES, LANES), p)
        yg = _experts(block_exp, nused, xg.reshape(p * SUBLANES, LANES),
                      exp_w_gu[l].astype(bf16), exp_w_down[l].astype(bf16))
        h = _combine(dest_blk, _tok_blocks(wts_t, G_TOK), cnt_tab, gst_tab, hmid.reshape(t, d), mod,
                     final_g.reshape(1, d), yg.reshape(p, SUBLANES, LANES),
                     l == n_layers - 1).reshape(bsz, s, d)
    return h
```

```python
import functools

import jax
import jax.numpy as jnp
import numpy as np
from jax import lax
from jax.experimental import pallas as pl
from jax.experimental.pallas import tpu as pltpu

EPS = 1e-6
CHUNK = 64
CONV_W = 4
RG_C = 8.0
RNN_BLOCKS = 8
GLA_HEADS = 4
GLA_DK = 128
GLA_DV = 256
GLA_RANK = 16
GLA_TAU = 16.0
N_EXPERTS = 128
TOP_K = 8
D_EXPERT = 256
ROUTED_SCALE = 2.5

LANES = 128
SUBLANES = 8
VMEM_LIMIT = 56 * 1024 * 1024

TM_IN = 256
TS_RNN = 256
TS_GLA = 256
TM_POST = 256
G_TOK = 512
R_EXP = 512
EXP_SPLIT = 1

bf16 = jnp.bfloat16
f32 = jnp.float32


def _cparams(n_axes):
    return pltpu.CompilerParams(
        dimension_semantics=("arbitrary",) * n_axes,
        vmem_limit_bytes=VMEM_LIMIT,
    )


def _dot(a, b):
    return jnp.dot(a, b, preferred_element_type=f32)


def _dot_nt(a, b):
    return lax.dot_general(a, b, (((1,), (1,)), ((), ())), preferred_element_type=f32)


def _dot_tn(a, b):
    return lax.dot_general(a, b, (((0,), (0,)), ((), ())), preferred_element_type=f32)


def _split_bf16(x):
    hi = x.astype(bf16)
    lo = (x - hi.astype(f32)).astype(bf16)
    return hi, lo


def _rmsnorm_rows(x, g):
    return x * lax.rsqrt(jnp.mean(x * x, axis=-1, keepdims=True) + EPS) * g


def _log_sigmoid(x):
    return jnp.minimum(x, 0.0) - jnp.log1p(jnp.exp(-jnp.abs(x)))


def _softplus(x):
    return jnp.maximum(x, 0.0) + jnp.log1p(jnp.exp(-jnp.abs(x)))


def _ada_kernel(c_ref, w_ref, b_ref, o_ref):
    c_act = jax.nn.silu(c_ref[...])
    c_hi, c_lo = _split_bf16(c_act)
    w_hi, w_lo = _split_bf16(w_ref[...])
    acc = _dot(c_hi, w_hi) + _dot(c_lo, w_hi) + _dot(c_hi, w_lo)
    o_ref[...] = acc + b_ref[...]


def _ada(c, ada_w, ada_b):
    n_layers, d, n6 = ada_w.shape
    bsz = c.shape[0]
    tn = 512
    return pl.pallas_call(
        _ada_kernel,
        grid=(n_layers, n6 // tn),
        in_specs=[
            pl.BlockSpec((bsz, d), lambda l, j: (0, 0)),
            pl.BlockSpec((None, d, tn), lambda l, j: (l, 0, j)),
            pl.BlockSpec((None, 1, tn), lambda l, j: (l, 0, j)),
        ],
        out_specs=pl.BlockSpec((None, bsz, tn), lambda l, j: (l, 0, j)),
        out_shape=jax.ShapeDtypeStruct((n_layers, bsz, n6), f32),
        compiler_params=_cparams(2),
        name="ada_mod",
    )(c, ada_w, ada_b.reshape(n_layers, 1, n6))


def _in_kernel(h_ref, mod_ref, g_ref, w_ref, wlr_ref, wup_ref, bup_ref,
               xr_ref, gg_ref, q_ref, k_ref, v_ref, og_ref, ga_ref, gb_ref, la_ref, u_scr):
    d = h_ref.shape[-1]
    sh1 = mod_ref[:, 0:d]
    sc1 = mod_ref[:, d:2 * d]
    u = _rmsnorm_rows(h_ref[...], g_ref[...]) * (1.0 + sc1) + sh1
    u_scr[...] = u.astype(bf16)

    def proj(c0, width):
        return _dot(u_scr[...], w_ref[:, c0:c0 + width])

    cw = 512
    for c in range(d // cw):
        xr_ref[:, c * cw:(c + 1) * cw] = proj(c * cw, cw)
    for c in range(d // cw):
        y = proj(d + c * cw, cw)
        gg_ref[:, c * cw:(c + 1) * cw] = jax.nn.gelu(y, approximate=True).astype(bf16)
    dqk = GLA_HEADS * GLA_DK
    q_ref[...] = (proj(2 * d, dqk) * (GLA_DK ** -0.5)).astype(bf16)
    k_ref[...] = proj(2 * d + dqk, dqk).astype(bf16)
    base = 2 * d + 2 * dqk
    for c in range(d // cw):
        v_ref[:, c * cw:(c + 1) * cw] = proj(base + c * cw, cw).astype(bf16)
    base += d
    for c in range(d // cw):
        y = proj(base + c * cw, cw)
        og_ref[:, c * cw:(c + 1) * cw] = jax.nn.silu(y).astype(bf16)
    base += d
    for c in range(d // cw):
        y = proj(base + c * cw, cw)
        ga_ref[:, c * cw:(c + 1) * cw] = jax.nn.sigmoid(y).astype(bf16)
    base += d
    for c in range(d // cw):
        y = proj(base + c * cw, cw)
        gb_ref[:, c * cw:(c + 1) * cw] = jax.nn.sigmoid(y).astype(bf16)
    lr = _dot(u_scr[...], wlr_ref[...]).astype(bf16)
    logit = _dot(lr, wup_ref[...]) + bup_ref[...]
    la_ref[...] = _log_sigmoid(logit) * (1.0 / GLA_TAU)


def _in_proj(h, mod, norm_g, w_main, w_lr, w_up, b_up):
    bsz, s, d = h.shape
    tm = TM_IN
    dqk = GLA_HEADS * GLA_DK
    row = lambda width: pl.BlockSpec((None, tm, width), lambda b, i: (b, i, 0))
    full = lambda a: pl.BlockSpec(a.shape, lambda b, i: (0,) * a.ndim)
    out_shapes = [
        jax.ShapeDtypeStruct((bsz, s, d), f32),
        jax.ShapeDtypeStruct((bsz, s, d), bf16),
        jax.ShapeDtypeStruct((bsz, s, dqk), bf16),
        jax.ShapeDtypeStruct((bsz, s, dqk), bf16),
        jax.ShapeDtypeStruct((bsz, s, d), bf16),
        jax.ShapeDtypeStruct((bsz, s, d), bf16),
        jax.ShapeDtypeStruct((bsz, s, d), bf16),
        jax.ShapeDtypeStruct((bsz, s, d), bf16),
        jax.ShapeDtypeStruct((bsz, s, dqk), f32),
    ]
    return pl.pallas_call(
        _in_kernel,
        grid=(bsz, s // tm),
        in_specs=[
            row(d),
            pl.BlockSpec((None, 1, mod.shape[-1]), lambda b, i: (b, 0, 0)),
            full(norm_g),
            pl.BlockSpec(w_main.shape, lambda b, i: (0, 0), pipeline_mode=pl.Buffered(1)),
            full(w_lr), full(w_up), full(b_up),
        ],
        out_specs=[row(d), row(d), row(dqk), row(dqk), row(d), row(d), row(d), row(d), row(dqk)],
        out_shape=out_shapes,
        scratch_shapes=[pltpu.VMEM((tm, d), bf16)],
        compiler_params=_cparams(2),
        name="in_proj",
    )(h, mod, norm_g, w_main, w_lr, w_up, b_up)


def _rglru_kernel(xr_ref, gg_ref, cw_ref, cb_ref, wab_ref, ba_ref, bx_ref, lam_ref,
                  o_ref, xpad, hcar):
    ts, d = xr_ref.shape
    si = pl.program_id(1)

    @pl.when(si == 0)
    def _():
        xpad[0:SUBLANES, :] = jnp.zeros((SUBLANES, d), f32)
        hcar[...] = jnp.zeros_like(hcar)

    xpad[SUBLANES:SUBLANES + ts, :] = xr_ref[...]
    xc = cb_ref[...] + xpad[SUBLANES:SUBLANES + ts, :] * cw_ref[CONV_W - 1:CONV_W, :]
    for j in range(CONV_W - 1):
        off = SUBLANES - (CONV_W - 1) + j
        xc = xc + xpad[off:off + ts, :] * cw_ref[j:j + 1, :]
    xpad[0:SUBLANES, :] = xpad[ts:ts + SUBLANES, :]

    groups = ts // SUBLANES
    bw = d // RNN_BLOCKS
    sub = lax.broadcasted_iota(jnp.int32, (groups, SUBLANES, bw), 1)
    first_row = (lax.broadcasted_iota(jnp.int32, (ts, bw), 0) == 0) & (si == 0)
    sp = _softplus(-lam_ref[...])
    xcb = xc.astype(bf16)
    for hb in range(RNN_BLOCKS):
        cs = slice(hb * bw, (hb + 1) * bw)
        x_h = xc[:, cs]
        g = _dot(xcb[:, cs], wab_ref[hb])
        r = jax.nn.sigmoid(g[:, :bw] + ba_ref[:, cs])
        i = jax.nn.sigmoid(g[:, bw:] + bx_ref[:, cs])
        log_a = (-RG_C) * r * sp[:, cs]
        a = jnp.exp(log_a)
        mult = jnp.sqrt(1.0 - a * a)
        mult = jnp.where(first_row, 1.0, mult)
        bt = mult * (i * x_h)
        av = a.reshape(groups, SUBLANES, bw)
        bv = bt.reshape(groups, SUBLANES, bw)
        for sh in (1, 2, 4):
            a_prev = pltpu.roll(av, sh, 1)
            b_prev = pltpu.roll(bv, sh, 1)
            keep = sub >= sh
            bv = jnp.where(keep, av * b_prev + bv, bv)
            av = jnp.where(keep, av * a_prev, av)
        hin = hcar[0:1, cs]
        outs = []
        for gi in range(groups):
            hg = av[gi] * hin + bv[gi]
            outs.append(hg)
            hin = hg[SUBLANES - 1:SUBLANES, :]
        hcar[0:1, cs] = hin
        hs = jnp.concatenate(outs, axis=0)
        o_ref[:, cs] = (hs * gg_ref[:, cs].astype(f32)).astype(bf16)


def _rglru(xr, gg, conv_w, conv_b, wab, ba, bx, lam):
    bsz, s, d = xr.shape
    ts = TS_RNN
    row = pl.BlockSpec((None, ts, d), lambda b, i: (b, i, 0))
    full = lambda a: pl.BlockSpec(a.shape, lambda b, i: (0,) * a.ndim)
    return pl.pallas_call(
        _rglru_kernel,
        grid=(bsz, s // ts),
        in_specs=[row, row, full(conv_w), full(conv_b), full(wab), full(ba), full(bx), full(lam)],
        out_specs=row,
        out_shape=jax.ShapeDtypeStruct((bsz, s, d), bf16),
        scratch_shapes=[pltpu.VMEM((ts + SUBLANES, d), f32), pltpu.VMEM((SUBLANES, d), f32)],
        compiler_params=_cparams(2),
        name="rglru",
    )(xr, gg, conv_w, conv_b, wab, ba, bx, lam)


def _gla_kernel(q_ref, k_ref, v_ref, la_ref, og_ref, g_ref, o_ref, st):
    ts = q_ref.shape[0]
    si = pl.program_id(1)

    @pl.when(si == 0)
    def _():
        st[...] = jnp.zeros_like(st)

    ri = lax.broadcasted_iota(jnp.int32, (CHUNK, CHUNK), 0)
    ci = lax.broadcasted_iota(jnp.int32, (CHUNK, CHUNK), 1)
    tri = jnp.where(ci <= ri, 1.0, 0.0).astype(bf16)
    for c in range(ts // CHUNK):
        rs = slice(c * CHUNK, (c + 1) * CHUNK)
        la_hi, la_lo = _split_bf16(la_ref[rs, :])
        cum = _dot(tri, la_hi) + _dot(tri, la_lo)
        tot = cum[CHUNK - 1:CHUNK, :]
        kdec = (k_ref[rs, :].astype(f32) * jnp.exp(tot - cum)).astype(bf16)
        dec = jnp.exp(tot)
        for hd in range(GLA_HEADS):
            ks = slice(hd * GLA_DK, (hd + 1) * GLA_DK)
            vs = slice(hd * GLA_DV, (hd + 1) * GLA_DV)
            s_new = st[hd] * dec[:, ks] + _dot_tn(v_ref[rs, vs], kdec[:, ks])
            st[hd] = s_new
            o = _dot_nt(q_ref[rs, ks], s_new.astype(bf16))
            o = _rmsnorm_rows(o, g_ref[:, vs])
            o_ref[rs, vs] = (o * og_ref[rs, vs].astype(f32)).astype(bf16)


def _gla(q, k, v, la, og, norm_g):
    bsz, s, dqk = q.shape
    d = v.shape[-1]
    ts = TS_GLA
    row = lambda width: pl.BlockSpec((None, ts, width), lambda b, i: (b, i, 0))
    return pl.pallas_call(
        _gla_kernel,
        grid=(bsz, s // ts),
        in_specs=[row(dqk), row(dqk), row(d), row(dqk), row(d),
                  pl.BlockSpec(norm_g.shape, lambda b, i: (0, 0))],
        out_specs=row(d),
        out_shape=jax.ShapeDtypeStruct((bsz, s, d), bf16),
        scratch_shapes=[pltpu.VMEM((GLA_HEADS, GLA_DV, GLA_DK), f32)],
        compiler_params=_cparams(2),
        name="gla",
    )(q, k, v, la, og, norm_g)


def _post_kernel(ya_ref, ob_ref, ga_ref, gb_ref, h_ref, mod_ref, wa_ref, wb_ref, wo_ref, g2_ref,
                 wsgu_ref, wsd_ref, wrh_ref, wrl_ref, rb_ref,
                 hmid_ref, urow_ref, idx_ref, wts_ref, rank_ref, cnt_ref, cnt):
    tm, d = h_ref.shape
    group_start = pl.program_id(1) % (G_TOK // tm) == 0

    @pl.when(group_start)
    def _():
        cnt[...] = jnp.zeros_like(cnt)

    g1 = mod_ref[:, 2 * d:3 * d]
    sh2 = mod_ref[:, 3 * d:4 * d]
    sc2 = mod_ref[:, 4 * d:5 * d]
    g2 = mod_ref[:, 5 * d:6 * d]

    ya = _dot(ya_ref[...], wa_ref[...])
    yb = _dot(ob_ref[...], wb_ref[...])
    m = (ga_ref[...].astype(f32) * ya + gb_ref[...].astype(f32) * yb).astype(bf16)
    h1 = h_ref[...] + g1 * _dot(m, wo_ref[...])
    u2 = _rmsnorm_rows(h1, g2_ref[...]) * (1.0 + sc2) + sh2
    u_hi, u_lo = _split_bf16(u2)

    gu = _dot(u_hi, wsgu_ref[...])
    dsh = wsd_ref.shape[0]
    act = (jax.nn.silu(gu[:, :dsh]) * gu[:, dsh:]).astype(bf16)
    hmid_ref[...] = h1 + g2 * _dot(act, wsd_ref[...])

    for j in range(d // LANES):
        urow_ref[pl.ds(j, tm, stride=SUBLANES), :] = u2[:, j * LANES:(j + 1) * LANES]

    logits = _dot_nt(wrh_ref[...], u_hi) + _dot_nt(wrh_ref[...], u_lo) + _dot_nt(wrl_ref[...], u_hi)
    scores = jax.nn.sigmoid(logits)
    work = scores + rb_ref[...]
    n_e = scores.shape[0]
    e_iota = lax.broadcasted_iota(jnp.int32, (n_e, tm), 0).astype(f32)
    onehots, idxs, svals = [], [], []
    for _ in range(TOP_K):
        mx = jnp.max(work, axis=0, keepdims=True)
        ik = jnp.min(jnp.where(work == mx, e_iota, float(n_e)), axis=0, keepdims=True)
        oh = e_iota == ik
        onehots.append(oh)
        idxs.append(ik.astype(jnp.int32))
        svals.append(jnp.sum(jnp.where(oh, scores, 0.0), axis=0, keepdims=True))
        work = jnp.where(oh, -jnp.inf, work)
    ssum = svals[0]
    for sv in svals[1:]:
        ssum = ssum + sv
    idx_ref[...] = jnp.concatenate(idxs, axis=0)
    wts_ref[...] = jnp.concatenate([sv / ssum * ROUTED_SCALE for sv in svals], axis=0)

    member = jnp.where(onehots[0], 1.0, 0.0)
    for oh in onehots[1:]:
        member = member + jnp.where(oh, 1.0, 0.0)
    ri = lax.broadcasted_iota(jnp.int32, (tm, tm), 0)
    ci = lax.broadcasted_iota(jnp.int32, (tm, tm), 1)
    upper = jnp.where(ri <= ci, 1.0, 0.0).astype(bf16)
    incl = _dot(member.astype(bf16), upper)
    base = jnp.concatenate([cnt[...]] * (tm // LANES), axis=1)
    pos = base + incl - member
    ranks = [jnp.sum(jnp.where(oh, pos, 0.0), axis=0, keepdims=True) for oh in onehots]
    rank_ref[...] = jnp.concatenate(ranks, axis=0).astype(jnp.int32)
    new_cnt = cnt[...] + jnp.broadcast_to(incl[:, tm - 1:tm], cnt.shape)
    cnt[...] = new_cnt
    cnt_ref[...] = new_cnt


def _post(ya_in, ob, ga, gb, h, mod, wa, wb, wo, g2, wsgu, wsd, wr_hi, wr_lo, rb):
    bsz, s, d = h.shape
    tm = TM_POST
    t = bsz * s
    nsb = s // tm
    tpg = G_TOK // tm
    n_groups = t // G_TOK
    row = pl.BlockSpec((None, tm, d), lambda b, i: (b, i, 0))
    full = lambda a: pl.BlockSpec(a.shape, lambda b, i: (0,) * a.ndim)
    tok = pl.BlockSpec((TOP_K, tm), lambda b, i: (0, b * nsb + i))
    return pl.pallas_call(
        _post_kernel,
        grid=(bsz, nsb),
        in_specs=[row, row, row, row, row,
                  pl.BlockSpec((None, 1, mod.shape[-1]), lambda b, i: (b, 0, 0)),
                  full(wa), full(wb), full(wo), full(g2), full(wsgu), full(wsd),
                  full(wr_hi), full(wr_lo), full(rb)],
        out_specs=[row,
                   pl.BlockSpec((tm * SUBLANES, LANES), lambda b, i: (b * nsb + i, 0)),
                   tok, tok, tok,
                   pl.BlockSpec((None, N_EXPERTS, LANES),
                                lambda b, i: ((b * nsb + i) // tpg, 0, 0))],
        out_shape=[jax.ShapeDtypeStruct((bsz, s, d), f32),
                   jax.ShapeDtypeStruct((t * SUBLANES, LANES), f32),
                   jax.ShapeDtypeStruct((TOP_K, t), jnp.int32),
                   jax.ShapeDtypeStruct((TOP_K, t), f32),
                   jax.ShapeDtypeStruct((TOP_K, t), jnp.int32),
                   jax.ShapeDtypeStruct((n_groups, N_EXPERTS, LANES), f32)],
        scratch_shapes=[pltpu.VMEM((N_EXPERTS, LANES), f32)],
        compiler_params=_cparams(2),
        name="post",
    )(ya_in, ob, ga, gb, h, mod, wa, wb, wo, g2, wsgu, wsd, wr_hi, wr_lo, rb)


def _dest_kernel(idx_ref, rank_ref, pstart_ref, o_ref):
    tn = idx_ref.shape[1]
    n_e = pstart_ref.shape[0]
    e_iota = lax.broadcasted_iota(jnp.int32, (n_e, tn), 0)
    rows = []
    for k in range(TOP_K):
        start = jnp.sum(jnp.where(e_iota == idx_ref[k:k + 1, :], pstart_ref[...], 0.0),
                        axis=0, keepdims=True)
        rows.append(start.astype(jnp.int32) + rank_ref[k:k + 1, :])
    o_ref[...] = jnp.concatenate(rows, axis=0)


def _dest(idx_t, rank_t, lstart_col):
    t = idx_t.shape[1]
    tn = G_TOK
    tok = pl.BlockSpec((TOP_K, tn), lambda i: (0, i))
    return pl.pallas_call(
        _dest_kernel,
        grid=(t // tn,),
        in_specs=[tok, tok, pl.BlockSpec((None, N_EXPERTS, 1), lambda i: (i, 0, 0))],
        out_specs=tok,
        out_shape=jax.ShapeDtypeStruct((TOP_K, t), jnp.int32),
        compiler_params=_cparams(1),
        name="dest",
    )(idx_t, rank_t, lstart_col)


TOK_UNROLL = 4


def _disp_kernel(dest_ref, cnt_ref, gst_ref, pads_ref, padn_ref, nused_ref, u_ref, xg_ref,
                 xloc, zbuf, sem):
    g_tok = u_ref.shape[0]
    n_rows = g_tok * TOP_K
    g = pl.program_id(0)
    last = pl.num_programs(0) - 1
    slot = g % 2

    def drain(s):
        pltpu.make_async_copy(xloc.at[s], xg_ref.at[pl.ds(0, n_rows)], sem.at[s]).wait()

    @pl.when(g == 0)
    def _():
        zbuf[...] = jnp.zeros_like(zbuf)

    @pl.when(g >= 2)
    def _():
        drain(slot)

    xs = xloc.at[slot]

    def fill(i, carry):
        for uu in range(TOK_UNROLL):
            t = i * TOK_UNROLL + uu
            row = u_ref[t]
            for k in range(TOP_K):
                xs[dest_ref[k, t]] = row
        return carry

    lax.fori_loop(0, g_tok // TOK_UNROLL, fill, 0)

    def seg(e, ls):
        c = cnt_ref[0, e]

        @pl.when(c > 0)
        def _():
            pltpu.make_async_copy(xs.at[pl.ds(ls, c)], xg_ref.at[pl.ds(gst_ref[0, e], c)],
                                  sem.at[slot]).start()

        return ls + c

    lax.fori_loop(0, N_EXPERTS, seg, 0)

    @pl.when(g == last)
    def _():
        def pad_copy(e):
            n = padn_ref[0, e]
            return pltpu.make_async_copy(zbuf.at[pl.ds(0, n)], xg_ref.at[pl.ds(pads_ref[0, e], n)],
                                         sem.at[2])

        def pad_start(e, carry):
            @pl.when(padn_ref[0, e] > 0)
            def _():
                pad_copy(e).start()
            return carry

        def pad_wait(e, carry):
            @pl.when(padn_ref[0, e] > 0)
            def _():
                pad_copy(e).wait()
            return carry

        n_blocks = xg_ref.shape[0] // R_EXP

        def tail_copy(i):
            return pltpu.make_async_copy(
                zbuf, xg_ref.at[pl.ds(pl.multiple_of(i * R_EXP, R_EXP), R_EXP)], sem.at[2])

        def tail_start(i, carry):
            tail_copy(i).start()
            return carry

        def tail_wait(i, carry):
            tail_copy(i).wait()
            return carry

        lax.fori_loop(0, N_EXPERTS, pad_start, 0)
        lax.fori_loop(nused_ref[0, 0], n_blocks, tail_start, 0)
        lax.fori_loop(0, N_EXPERTS, pad_wait, 0)
        lax.fori_loop(nused_ref[0, 0], n_blocks, tail_wait, 0)
        drain(1 - slot)
        drain(slot)


def _dispatch(dest_blk, cnt_tab, gst_tab, pad_start, pad_len, nused, urows, p_rows):
    t = urows.shape[0]
    n_groups = t // G_TOK
    assert n_groups >= 2
    smem_grp = lambda: pl.BlockSpec((None, 1, N_EXPERTS), lambda i: (i, 0, 0),
                                    memory_space=pltpu.SMEM)
    smem_all = lambda: pl.BlockSpec((1, N_EXPERTS), lambda i: (0, 0), memory_space=pltpu.SMEM)
    return pl.pallas_call(
        _disp_kernel,
        grid=(n_groups,),
        in_specs=[
            pl.BlockSpec((None, TOP_K, G_TOK), lambda i: (i, 0, 0), memory_space=pltpu.SMEM),
            smem_grp(), smem_grp(), smem_all(), smem_all(),
            pl.BlockSpec((1, 1), lambda i: (0, 0), memory_space=pltpu.SMEM),
            pl.BlockSpec((G_TOK, SUBLANES, LANES), lambda i: (i, 0, 0)),
        ],
        out_specs=pl.BlockSpec(memory_space=pl.ANY),
        out_shape=jax.ShapeDtypeStruct((p_rows, SUBLANES, LANES), f32),
        scratch_shapes=[pltpu.VMEM((2, G_TOK * TOP_K, SUBLANES, LANES), f32),
                        pltpu.VMEM((R_EXP, SUBLANES, LANES), f32),
                        pltpu.SemaphoreType.DMA((3,))],
        compiler_params=_cparams(1),
        name="dispatch",
    )(dest_blk, cnt_tab, gst_tab, pad_start, pad_len, nused.reshape(1, 1), urows)


def _experts_kernel(be_ref, nused_ref, x_ref, wgu_ref, wd_ref, y_ref, wgu_b, wd_b):
    rows = x_ref.shape[0] // SUBLANES
    d = wgu_ref.shape[0]
    i = pl.program_id(0)

    @pl.when((i == 0) | (be_ref[i] != be_ref[jnp.maximum(i - 1, 0)]))
    def _():
        wgu_b[...] = wgu_ref[...].astype(bf16)
        wd_b[...] = wd_ref[...].astype(bf16)

    @pl.when(i < nused_ref[0])
    def _():
        sub = rows // EXP_SPLIT
        for hf in range(EXP_SPLIT):
            base = hf * sub * SUBLANES
            x = jnp.concatenate(
                [x_ref[pl.ds(base + j, sub, stride=SUBLANES), :] for j in range(d // LANES)],
                axis=1).astype(bf16)
            gu = _dot(x, wgu_b[...])
            de = wd_b.shape[0]
            act = (jax.nn.silu(gu[:, :de]) * gu[:, de:]).astype(bf16)
            y = _dot(act, wd_b[...])
            for j in range(d // LANES):
                y_ref[pl.ds(base + j, sub, stride=SUBLANES), :] = y[:, j * LANES:(j + 1) * LANES]

    @pl.when(i >= nused_ref[0])
    def _():
        y_ref[...] = jnp.zeros_like(y_ref)


def _experts(block_exp, nused, xg2, w_gu, w_down):
    p8 = xg2.shape[0]
    n_e, d, de2 = w_gu.shape
    rb = R_EXP * SUBLANES
    nb = p8 // rb
    grid_spec = pltpu.PrefetchScalarGridSpec(
        num_scalar_prefetch=2,
        grid=(nb,),
        in_specs=[
            pl.BlockSpec((rb, LANES), lambda i, be, nu: (jnp.minimum(i, nu[0] - 1), 0)),
            pl.BlockSpec((None, d, de2), lambda i, be, nu: (be[i], 0, 0)),
            pl.BlockSpec((None, de2 // 2, d), lambda i, be, nu: (be[i], 0, 0)),
        ],
        out_specs=pl.BlockSpec((rb, LANES), lambda i, be, nu: (i, 0)),
        scratch_shapes=[pltpu.VMEM((d, de2), bf16), pltpu.VMEM((de2 // 2, d), bf16)],
    )
    return pl.pallas_call(
        _experts_kernel,
        grid_spec=grid_spec,
        out_shape=jax.ShapeDtypeStruct((p8, LANES), f32),
        compiler_params=_cparams(1),
        name="experts",
    )(block_exp, nused, xg2, w_gu, w_down)


def _comb_kernel(dest_ref, wts_ref, cnt_ref, gst_ref, cntn_ref, gstn_ref, hmid_ref, mod_ref, fg_ref,
                 yg_ref, o_ref, yloc, acc, sem, *, final):
    g_tok, d = hmid_ref.shape
    n_rows = g_tok * TOP_K
    g = pl.program_id(0)
    n_groups = pl.num_programs(0)
    slot = g % 2

    def fetch(cnt_r, gst_r, s):
        def seg(e, ls):
            c = cnt_r[0, e]

            @pl.when(c > 0)
            def _():
                pltpu.make_async_copy(yg_ref.at[pl.ds(gst_r[0, e], c)],
                                      yloc.at[s, pl.ds(ls, c)], sem.at[s]).start()

            return ls + c

        lax.fori_loop(0, N_EXPERTS, seg, 0)

    @pl.when(g == 0)
    def _():
        fetch(cnt_ref, gst_ref, 0)

    @pl.when(g + 1 < n_groups)
    def _():
        fetch(cntn_ref, gstn_ref, 1 - slot)

    pltpu.make_async_copy(yg_ref.at[pl.ds(0, n_rows)], yloc.at[slot], sem.at[slot]).wait()

    ys = yloc.at[slot]

    def reduce(i, carry):
        for uu in range(TOK_UNROLL):
            t = i * TOK_UNROLL + uu
            a = ys[dest_ref[0, t]] * wts_ref[0, t]
            for k in range(1, TOP_K):
                a = a + ys[dest_ref[k, t]] * wts_ref[k, t]
            acc[pl.ds(pl.multiple_of(t * SUBLANES, SUBLANES), SUBLANES), :] = a
        return carry

    lax.fori_loop(0, g_tok // TOK_UNROLL, reduce, 0)
    for j in range(d // LANES):
        cs = slice(j * LANES, (j + 1) * LANES)
        o_ref[:, cs] = hmid_ref[:, cs] + mod_ref[:, 5 * d + j * LANES:5 * d + (j + 1) * LANES] * \
            acc[pl.ds(j, g_tok, stride=SUBLANES), :]
    if final:
        o_ref[...] = _rmsnorm_rows(o_ref[...], fg_ref[...])


def _combine(dest_blk, wts_blk, cnt_tab, gst_tab, hmid, mod, final_g, yg, final):
    t, d = hmid.shape
    n_groups = t // G_TOK
    gpb = n_groups // mod.shape[0]
    row = pl.BlockSpec((G_TOK, d), lambda i: (i, 0))
    smem_tok = lambda: pl.BlockSpec((None, TOP_K, G_TOK), lambda i: (i, 0, 0),
                                    memory_space=pltpu.SMEM)
    smem_cur = lambda: pl.BlockSpec((None, 1, N_EXPERTS), lambda i: (i, 0, 0),
                                    memory_space=pltpu.SMEM)
    smem_nxt = lambda: pl.BlockSpec((None, 1, N_EXPERTS),
                                    lambda i: (jnp.minimum(i + 1, n_groups - 1), 0, 0),
                                    memory_space=pltpu.SMEM)
    return pl.pallas_call(
        functools.partial(_comb_kernel, final=final),
        grid=(n_groups,),
        in_specs=[smem_tok(), smem_tok(), smem_cur(), smem_cur(), smem_nxt(), smem_nxt(), row,
                  pl.BlockSpec((None, 1, mod.shape[-1]), lambda i: (i // gpb, 0, 0)),
                  pl.BlockSpec(final_g.shape, lambda i: (0, 0)),
                  pl.BlockSpec(memory_space=pl.ANY)],
        out_specs=row,
        out_shape=jax.ShapeDtypeStruct((t, d), f32),
        scratch_shapes=[pltpu.VMEM((2, G_TOK * TOP_K, SUBLANES, LANES), f32),
                        pltpu.VMEM((G_TOK * SUBLANES, LANES), f32),
                        pltpu.SemaphoreType.DMA((2,))],
        compiler_params=_cparams(1),
        name="combine_final" if final else "combine",
    )(dest_blk, wts_blk, cnt_tab, gst_tab, cnt_tab, gst_tab, hmid, mod, final_g, yg)


def _tok_blocks(a_t, blk):
    k, t = a_t.shape
    return a_t.reshape(k, t // blk, blk).transpose(1, 0, 2)


def kernel(x, c, ada_w, ada_b, norm1_g, w_in, conv_w, conv_b, rg_wa, rg_ba, rg_wx, rg_bx, rg_lambda,
           gla_w_up, gla_b_up, gla_norm_g, w_branch_a, w_branch_b, w_out, norm2_g, router_w, router_b,
           exp_w_gu, exp_w_down, shared_w_gu, shared_w_down, final_g):
    bsz, s, d = x.shape
    n_layers = ada_w.shape[0]
    t = bsz * s
    dqk = GLA_HEADS * GLA_DK
    n_assign = t * TOP_K
    nb = -(-(n_assign + N_EXPERTS * (R_EXP - 1)) // R_EXP)
    p = nb * R_EXP
    n_groups = t // G_TOK

    mod_all = _ada(c, ada_w, ada_b)
    h = x
    for l in range(n_layers):
        mod = mod_all[l].reshape(bsz, 1, 6 * d)
        wl = w_in[l]
        lr0 = 2 * d + 2 * dqk + 2 * d
        w_main = jnp.concatenate([wl[:, :lr0], wl[:, lr0 + GLA_RANK:]], axis=1).astype(bf16)
        w_lr = jnp.pad(wl[:, lr0:lr0 + GLA_RANK], ((0, 0), (0, LANES - GLA_RANK))).astype(bf16)
        w_up = jnp.pad(gla_w_up[l], ((0, LANES - GLA_RANK), (0, 0))).astype(bf16)
        xr, gg, q, k, v, og, ga, gb, la = _in_proj(
            h, mod, norm1_g[l].reshape(1, d), w_main, w_lr, w_up, gla_b_up[l].reshape(1, dqk))

        wab = jnp.concatenate([rg_wa[l], rg_wx[l]], axis=-1).astype(bf16)
        ya_in = _rglru(xr, gg, conv_w[l], conv_b[l].reshape(1, d), wab,
                       rg_ba[l].reshape(1, d), rg_bx[l].reshape(1, d), rg_lambda[l].reshape(1, d))
        ob = _gla(q, k, v, la, og, gla_norm_g[l].reshape(1, d))

        wr_t = router_w[l].T
        wr_hi = wr_t.astype(bf16)
        wr_lo = (wr_t - wr_hi.astype(f32)).astype(bf16)
        hmid, urows, idx_t, wts_t, rank_t, counts = _post(
            ya_in, ob, ga, gb, h, mod,
            w_branch_a[l].astype(bf16), w_branch_b[l].astype(bf16), w_out[l].astype(bf16),
            norm2_g[l].reshape(1, d), shared_w_gu[l].astype(bf16), shared_w_down[l].astype(bf16),
            wr_hi, wr_lo, router_b[l].reshape(N_EXPERTS, 1))

        cnt = counts[:, :, 0].astype(jnp.int32)
        lstart = jnp.cumsum(cnt, axis=1) - cnt
        ctot = jnp.sum(cnt, axis=0)
        padded = (ctot + R_EXP - 1) // R_EXP * R_EXP
        ends = jnp.cumsum(padded)
        pstart = ends - padded
        gstart = pstart[None, :] + jnp.cumsum(cnt, axis=0) - cnt
        blk_row = jnp.arange(nb, dtype=jnp.int32) * R_EXP
        block_exp = jnp.minimum(
            jnp.sum((ends[None, :] <= blk_row[:, None]).astype(jnp.int32), axis=1),
            N_EXPERTS - 1).astype(jnp.int32)
        nused = (ends[-1:] // R_EXP).astype(jnp.int32)
        cnt_tab = cnt.reshape(n_groups, 1, N_EXPERTS)
        gst_tab = gstart.astype(jnp.int32).reshape(n_groups, 1, N_EXPERTS)

        dest_t = _dest(idx_t, rank_t, lstart.astype(f32).reshape(n_groups, N_EXPERTS, 1))
        dest_blk = _tok_blocks(dest_t, G_TOK)
        xg = _dispatch(dest_blk, cnt_tab, gst_tab,
                       (pstart + ctot).astype(jnp.int32).reshape(1, N_EXPERTS),
                       (padded - ctot).astype(jnp.int32).reshape(1, N_EXPERTS),
                       nused, urows.reshape(t, SUBLANES, LANES), p)
        yg = _experts(block_exp, nused, xg.reshape(p * SUBLANES, LANES), exp_w_gu[l], exp_w_down[l])
        h = _combine(dest_blk, _tok_blocks(wts_t, G_TOK), cnt_tab, gst_tab, hmid.reshape(t, d), mod,
                     final_g.reshape(1, d), yg.reshape(p, SUBLANES, LANES),
                     l == n_layers - 1).reshape(bsz, s, d)
    return h
```

```python
import functools

import jax
import jax.numpy as jnp
import numpy as np
from jax import lax
from jax.experimental import pallas as pl
from jax.experimental.pallas import tpu as pltpu

EPS = 1e-6
CHUNK = 64
CONV_W = 4
RG_C = 8.0
RNN_BLOCKS = 8
GLA_HEADS = 4
GLA_DK = 128
GLA_DV = 256
GLA_RANK = 16
GLA_TAU = 16.0
N_EXPERTS = 128
TOP_K = 8
D_EXPERT = 256
ROUTED_SCALE = 2.5

LANES = 128
SUBLANES = 8
VMEM_LIMIT = 56 * 1024 * 1024

TM_IN = 256
TS_RNN = 256
TS_GLA = 256
TM_POST = 256
G_TOK = 512
R_EXP = 512
EXP_SPLIT = 1

bf16 = jnp.bfloat16
f32 = jnp.float32


def _cparams(n_axes):
    return pltpu.CompilerParams(
        dimension_semantics=("arbitrary",) * n_axes,
        vmem_limit_bytes=VMEM_LIMIT,
    )


def _dot(a, b):
    return jnp.dot(a, b, preferred_element_type=f32)


def _dot_nt(a, b):
    return lax.dot_general(a, b, (((1,), (1,)), ((), ())), preferred_element_type=f32)


def _dot_tn(a, b):
    return lax.dot_general(a, b, (((0,), (0,)), ((), ())), preferred_element_type=f32)


def _split_bf16(x):
    hi = x.astype(bf16)
    lo = (x - hi.astype(f32)).astype(bf16)
    return hi, lo


def _rmsnorm_rows(x, g):
    return x * lax.rsqrt(jnp.mean(x * x, axis=-1, keepdims=True) + EPS) * g


def _log_sigmoid(x):
    return jnp.minimum(x, 0.0) - jnp.log1p(jnp.exp(-jnp.abs(x)))


def _softplus(x):
    return jnp.maximum(x, 0.0) + jnp.log1p(jnp.exp(-jnp.abs(x)))


def _ada_kernel(c_ref, w_ref, b_ref, o_ref):
    c_act = jax.nn.silu(c_ref[...])
    c_hi, c_lo = _split_bf16(c_act)
    w_hi, w_lo = _split_bf16(w_ref[...])
    acc = _dot(c_hi, w_hi) + _dot(c_lo, w_hi) + _dot(c_hi, w_lo)
    o_ref[...] = acc + b_ref[...]


def _ada(c, ada_w, ada_b):
    n_layers, d, n6 = ada_w.shape
    bsz = c.shape[0]
    tn = 512
    return pl.pallas_call(
        _ada_kernel,
        grid=(n_layers, n6 // tn),
        in_specs=[
            pl.BlockSpec((bsz, d), lambda l, j: (0, 0)),
            pl.BlockSpec((None, d, tn), lambda l, j: (l, 0, j)),
            pl.BlockSpec((None, 1, tn), lambda l, j: (l, 0, j)),
        ],
        out_specs=pl.BlockSpec((None, bsz, tn), lambda l, j: (l, 0, j)),
        out_shape=jax.ShapeDtypeStruct((n_layers, bsz, n6), f32),
        compiler_params=_cparams(2),
        name="ada_mod",
    )(c, ada_w, ada_b.reshape(n_layers, 1, n6))


def _in_kernel(h_ref, mod_ref, g_ref, w_ref, wlr_ref, wup_ref, bup_ref,
               xr_ref, gg_ref, q_ref, k_ref, v_ref, og_ref, ga_ref, gb_ref, la_ref, u_scr):
    d = h_ref.shape[-1]
    sh1 = mod_ref[:, 0:d]
    sc1 = mod_ref[:, d:2 * d]
    u = _rmsnorm_rows(h_ref[...], g_ref[...]) * (1.0 + sc1) + sh1
    u_scr[...] = u.astype(bf16)

    def proj(c0, width):
        return _dot(u_scr[...], w_ref[:, c0:c0 + width])

    cw = 512
    for c in range(d // cw):
        xr_ref[:, c * cw:(c + 1) * cw] = proj(c * cw, cw)
    for c in range(d // cw):
        y = proj(d + c * cw, cw)
        gg_ref[:, c * cw:(c + 1) * cw] = jax.nn.gelu(y, approximate=True).astype(bf16)
    dqk = GLA_HEADS * GLA_DK
    q_ref[...] = (proj(2 * d, dqk) * (GLA_DK ** -0.5)).astype(bf16)
    k_ref[...] = proj(2 * d + dqk, dqk).astype(bf16)
    base = 2 * d + 2 * dqk
    for c in range(d // cw):
        v_ref[:, c * cw:(c + 1) * cw] = proj(base + c * cw, cw).astype(bf16)
    base += d
    for c in range(d // cw):
        y = proj(base + c * cw, cw)
        og_ref[:, c * cw:(c + 1) * cw] = jax.nn.silu(y).astype(bf16)
    base += d
    for c in range(d // cw):
        y = proj(base + c * cw, cw)
        ga_ref[:, c * cw:(c + 1) * cw] = jax.nn.sigmoid(y).astype(bf16)
    base += d
    for c in range(d // cw):
        y = proj(base + c * cw, cw)
        gb_ref[:, c * cw:(c + 1) * cw] = jax.nn.sigmoid(y).astype(bf16)
    lr = _dot(u_scr[...], wlr_ref[...]).astype(bf16)
    logit = _dot(lr, wup_ref[...]) + bup_ref[...]
    la_ref[...] = _log_sigmoid(logit) * (1.0 / GLA_TAU)


def _in_proj(h, mod, norm_g, w_main, w_lr, w_up, b_up):
    bsz, s, d = h.shape
    tm = TM_IN
    dqk = GLA_HEADS * GLA_DK
    row = lambda width: pl.BlockSpec((None, tm, width), lambda b, i: (b, i, 0))
    full = lambda a: pl.BlockSpec(a.shape, lambda b, i: (0,) * a.ndim)
    out_shapes = [
        jax.ShapeDtypeStruct((bsz, s, d), f32),
        jax.ShapeDtypeStruct((bsz, s, d), bf16),
        jax.ShapeDtypeStruct((bsz, s, dqk), bf16),
        jax.ShapeDtypeStruct((bsz, s, dqk), bf16),
        jax.ShapeDtypeStruct((bsz, s, d), bf16),
        jax.ShapeDtypeStruct((bsz, s, d), bf16),
        jax.ShapeDtypeStruct((bsz, s, d), bf16),
        jax.ShapeDtypeStruct((bsz, s, d), bf16),
        jax.ShapeDtypeStruct((bsz, s, dqk), f32),
    ]
    return pl.pallas_call(
        _in_kernel,
        grid=(bsz, s // tm),
        in_specs=[
            row(d),
            pl.BlockSpec((None, 1, mod.shape[-1]), lambda b, i: (b, 0, 0)),
            full(norm_g),
            pl.BlockSpec(w_main.shape, lambda b, i: (0, 0), pipeline_mode=pl.Buffered(1)),
            full(w_lr), full(w_up), full(b_up),
        ],
        out_specs=[row(d), row(d), row(dqk), row(dqk), row(d), row(d), row(d), row(d), row(dqk)],
        out_shape=out_shapes,
        scratch_shapes=[pltpu.VMEM((tm, d), bf16)],
        compiler_params=_cparams(2),
        name="in_proj",
    )(h, mod, norm_g, w_main, w_lr, w_up, b_up)


def _rglru_kernel(xr_ref, gg_ref, cw_ref, cb_ref, wab_ref, ba_ref, bx_ref, lam_ref,
                  o_ref, xpad, hcar):
    ts, d = xr_ref.shape
    si = pl.program_id(1)

    @pl.when(si == 0)
    def _():
        xpad[0:SUBLANES, :] = jnp.zeros((SUBLANES, d), f32)
        hcar[...] = jnp.zeros_like(hcar)

    xpad[SUBLANES:SUBLANES + ts, :] = xr_ref[...]
    xc = cb_ref[...] + xpad[SUBLANES:SUBLANES + ts, :] * cw_ref[CONV_W - 1:CONV_W, :]
    for j in range(CONV_W - 1):
        off = SUBLANES - (CONV_W - 1) + j
        xc = xc + xpad[off:off + ts, :] * cw_ref[j:j + 1, :]
    xpad[0:SUBLANES, :] = xpad[ts:ts + SUBLANES, :]

    groups = ts // SUBLANES
    bw = d // RNN_BLOCKS
    sub = lax.broadcasted_iota(jnp.int32, (groups, SUBLANES, bw), 1)
    first_row = (lax.broadcasted_iota(jnp.int32, (ts, bw), 0) == 0) & (si == 0)
    sp = _softplus(-lam_ref[...])
    xcb = xc.astype(bf16)
    for hb in range(RNN_BLOCKS):
        cs = slice(hb * bw, (hb + 1) * bw)
        x_h = xc[:, cs]
        g = _dot(xcb[:, cs], wab_ref[hb])
        r = jax.nn.sigmoid(g[:, :bw] + ba_ref[:, cs])
        i = jax.nn.sigmoid(g[:, bw:] + bx_ref[:, cs])
        log_a = (-RG_C) * r * sp[:, cs]
        a = jnp.exp(log_a)
        mult = jnp.sqrt(1.0 - a * a)
        mult = jnp.where(first_row, 1.0, mult)
        bt = mult * (i * x_h)
        av = a.reshape(groups, SUBLANES, bw)
        bv = bt.reshape(groups, SUBLANES, bw)
        for sh in (1, 2, 4):
            a_prev = pltpu.roll(av, sh, 1)
            b_prev = pltpu.roll(bv, sh, 1)
            keep = sub >= sh
            bv = jnp.where(keep, av * b_prev + bv, bv)
            av = jnp.where(keep, av * a_prev, av)
        hin = hcar[0:1, cs]
        outs = []
        for gi in range(groups):
            hg = av[gi] * hin + bv[gi]
            outs.append(hg)
            hin = hg[SUBLANES - 1:SUBLANES, :]
        hcar[0:1, cs] = hin
        hs = jnp.concatenate(outs, axis=0)
        o_ref[:, cs] = (hs * gg_ref[:, cs].astype(f32)).astype(bf16)


def _rglru(xr, gg, conv_w, conv_b, wab, ba, bx, lam):
    bsz, s, d = xr.shape
    ts = TS_RNN
    row = pl.BlockSpec((None, ts, d), lambda b, i: (b, i, 0))
    full = lambda a: pl.BlockSpec(a.shape, lambda b, i: (0,) * a.ndim)
    return pl.pallas_call(
        _rglru_kernel,
        grid=(bsz, s // ts),
        in_specs=[row, row, full(conv_w), full(conv_b), full(wab), full(ba), full(bx), full(lam)],
        out_specs=row,
        out_shape=jax.ShapeDtypeStruct((bsz, s, d), bf16),
        scratch_shapes=[pltpu.VMEM((ts + SUBLANES, d), f32), pltpu.VMEM((SUBLANES, d), f32)],
        compiler_params=_cparams(2),
        name="rglru",
    )(xr, gg, conv_w, conv_b, wab, ba, bx, lam)


def _gla_kernel(q_ref, k_ref, v_ref, la_ref, og_ref, g_ref, o_ref, st):
    ts = q_ref.shape[0]
    si = pl.program_id(1)

    @pl.when(si == 0)
    def _():
        st[...] = jnp.zeros_like(st)

    ri = lax.broadcasted_iota(jnp.int32, (CHUNK, CHUNK), 0)
    ci = lax.broadcasted_iota(jnp.int32, (CHUNK, CHUNK), 1)
    tri = jnp.where(ci <= ri, 1.0, 0.0).astype(bf16)
    for c in range(ts // CHUNK):
        rs = slice(c * CHUNK, (c + 1) * CHUNK)
        la_hi, la_lo = _split_bf16(la_ref[rs, :])
        cum = _dot(tri, la_hi) + _dot(tri, la_lo)
        tot = cum[CHUNK - 1:CHUNK, :]
        kdec = (k_ref[rs, :].astype(f32) * jnp.exp(tot - cum)).astype(bf16)
        dec = jnp.exp(tot)
        for hd in range(GLA_HEADS):
            ks = slice(hd * GLA_DK, (hd + 1) * GLA_DK)
            vs = slice(hd * GLA_DV, (hd + 1) * GLA_DV)
            s_new = st[hd] * dec[:, ks] + _dot_tn(v_ref[rs, vs], kdec[:, ks])
            st[hd] = s_new
            o = _dot_nt(q_ref[rs, ks], s_new.astype(bf16))
            o = _rmsnorm_rows(o, g_ref[:, vs])
            o_ref[rs, vs] = (o * og_ref[rs, vs].astype(f32)).astype(bf16)


def _gla(q, k, v, la, og, norm_g):
    bsz, s, dqk = q.shape
    d = v.shape[-1]
    ts = TS_GLA
    row = lambda width: pl.BlockSpec((None, ts, width), lambda b, i: (b, i, 0))
    return pl.pallas_call(
        _gla_kernel,
        grid=(bsz, s // ts),
        in_specs=[row(dqk), row(dqk), row(d), row(dqk), row(d),
                  pl.BlockSpec(norm_g.shape, lambda b, i: (0, 0))],
        out_specs=row(d),
        out_shape=jax.ShapeDtypeStruct((bsz, s, d), bf16),
        scratch_shapes=[pltpu.VMEM((GLA_HEADS, GLA_DV, GLA_DK), f32)],
        compiler_params=_cparams(2),
        name="gla",
    )(q, k, v, la, og, norm_g)


def _post_kernel(ya_ref, ob_ref, ga_ref, gb_ref, h_ref, mod_ref, wa_ref, wb_ref, wo_ref, g2_ref,
                 wsgu_ref, wsd_ref, wrh_ref, wrl_ref, rb_ref,
                 hmid_ref, urow_ref, idx_ref, wts_ref, rank_ref, cnt_ref, cnt):
    tm, d = h_ref.shape
    group_start = pl.program_id(1) % (G_TOK // tm) == 0

    @pl.when(group_start)
    def _():
        cnt[...] = jnp.zeros_like(cnt)

    g1 = mod_ref[:, 2 * d:3 * d]
    sh2 = mod_ref[:, 3 * d:4 * d]
    sc2 = mod_ref[:, 4 * d:5 * d]
    g2 = mod_ref[:, 5 * d:6 * d]

    ya = _dot(ya_ref[...], wa_ref[...])
    yb = _dot(ob_ref[...], wb_ref[...])
    m = (ga_ref[...].astype(f32) * ya + gb_ref[...].astype(f32) * yb).astype(bf16)
    h1 = h_ref[...] + g1 * _dot(m, wo_ref[...])
    u2 = _rmsnorm_rows(h1, g2_ref[...]) * (1.0 + sc2) + sh2
    u_hi, u_lo = _split_bf16(u2)

    gu = _dot(u_hi, wsgu_ref[...])
    dsh = wsd_ref.shape[0]
    act = (jax.nn.silu(gu[:, :dsh]) * gu[:, dsh:]).astype(bf16)
    hmid_ref[...] = h1 + g2 * _dot(act, wsd_ref[...])

    for j in range(d // LANES):
        urow_ref[pl.ds(j, tm, stride=SUBLANES), :] = u2[:, j * LANES:(j + 1) * LANES]

    logits = _dot_nt(wrh_ref[...], u_hi) + _dot_nt(wrh_ref[...], u_lo) + _dot_nt(wrl_ref[...], u_hi)
    scores = jax.nn.sigmoid(logits)
    work = scores + rb_ref[...]
    n_e = scores.shape[0]
    e_iota = lax.broadcasted_iota(jnp.int32, (n_e, tm), 0).astype(f32)
    onehots, idxs, svals = [], [], []
    for _ in range(TOP_K):
        mx = jnp.max(work, axis=0, keepdims=True)
        ik = jnp.min(jnp.where(work == mx, e_iota, float(n_e)), axis=0, keepdims=True)
        oh = e_iota == ik
        onehots.append(oh)
        idxs.append(ik.astype(jnp.int32))
        svals.append(jnp.sum(jnp.where(oh, scores, 0.0), axis=0, keepdims=True))
        work = jnp.where(oh, -jnp.inf, work)
    ssum = svals[0]
    for sv in svals[1:]:
        ssum = ssum + sv
    idx_ref[...] = jnp.concatenate(idxs, axis=0)
    wts_ref[...] = jnp.concatenate([sv / ssum * ROUTED_SCALE for sv in svals], axis=0)

    member = jnp.where(onehots[0], 1.0, 0.0)
    for oh in onehots[1:]:
        member = member + jnp.where(oh, 1.0, 0.0)
    ri = lax.broadcasted_iota(jnp.int32, (tm, tm), 0)
    ci = lax.broadcasted_iota(jnp.int32, (tm, tm), 1)
    upper = jnp.where(ri <= ci, 1.0, 0.0).astype(bf16)
    incl = _dot(member.astype(bf16), upper)
    base = jnp.concatenate([cnt[...]] * (tm // LANES), axis=1)
    pos = base + incl - member
    ranks = [jnp.sum(jnp.where(oh, pos, 0.0), axis=0, keepdims=True) for oh in onehots]
    rank_ref[...] = jnp.concatenate(ranks, axis=0).astype(jnp.int32)
    new_cnt = cnt[...] + jnp.broadcast_to(incl[:, tm - 1:tm], cnt.shape)
    cnt[...] = new_cnt
    cnt_ref[...] = new_cnt


def _post(ya_in, ob, ga, gb, h, mod, wa, wb, wo, g2, wsgu, wsd, wr_hi, wr_lo, rb):
    bsz, s, d = h.shape
    tm = TM_POST
    t = bsz * s
    nsb = s // tm
    tpg = G_TOK // tm
    n_groups = t // G_TOK
    row = pl.BlockSpec((None, tm, d), lambda b, i: (b, i, 0))
    full = lambda a: pl.BlockSpec(a.shape, lambda b, i: (0,) * a.ndim)
    tok = pl.BlockSpec((TOP_K, tm), lambda b, i: (0, b * nsb + i))
    return pl.pallas_call(
        _post_kernel,
        grid=(bsz, nsb),
        in_specs=[row, row, row, row, row,
                  pl.BlockSpec((None, 1, mod.shape[-1]), lambda b, i: (b, 0, 0)),
                  full(wa), full(wb), full(wo), full(g2), full(wsgu), full(wsd),
                  full(wr_hi), full(wr_lo), full(rb)],
        out_specs=[row,
                   pl.BlockSpec((tm * SUBLANES, LANES), lambda b, i: (b * nsb + i, 0)),
                   tok, tok, tok,
                   pl.BlockSpec((None, N_EXPERTS, LANES),
                                lambda b, i: ((b * nsb + i) // tpg, 0, 0))],
        out_shape=[jax.ShapeDtypeStruct((bsz, s, d), f32),
                   jax.ShapeDtypeStruct((t * SUBLANES, LANES), f32),
                   jax.ShapeDtypeStruct((TOP_K, t), jnp.int32),
                   jax.ShapeDtypeStruct((TOP_K, t), f32),
                   jax.ShapeDtypeStruct((TOP_K, t), jnp.int32),
                   jax.ShapeDtypeStruct((n_groups, N_EXPERTS, LANES), f32)],
        scratch_shapes=[pltpu.VMEM((N_EXPERTS, LANES), f32)],
        compiler_params=_cparams(2),
        name="post",
    )(ya_in, ob, ga, gb, h, mod, wa, wb, wo, g2, wsgu, wsd, wr_hi, wr_lo, rb)


def _dest_kernel(idx_ref, rank_ref, pstart_ref, o_ref):
    tn = idx_ref.shape[1]
    n_e = pstart_ref.shape[0]
    e_iota = lax.broadcasted_iota(jnp.int32, (n_e, tn), 0)
    rows = []
    for k in range(TOP_K):
        start = jnp.sum(jnp.where(e_iota == idx_ref[k:k + 1, :], pstart_ref[...], 0.0),
                        axis=0, keepdims=True)
        rows.append(start.astype(jnp.int32) + rank_ref[k:k + 1, :])
    o_ref[...] = jnp.concatenate(rows, axis=0) * SUBLANES


def _dest(idx_t, rank_t, lstart_col):
    t = idx_t.shape[1]
    tn = G_TOK
    tok = pl.BlockSpec((TOP_K, tn), lambda i: (0, i))
    return pl.pallas_call(
        _dest_kernel,
        grid=(t // tn,),
        in_specs=[tok, tok, pl.BlockSpec((None, N_EXPERTS, 1), lambda i: (i, 0, 0))],
        out_specs=tok,
        out_shape=jax.ShapeDtypeStruct((TOP_K, t), jnp.int32),
        compiler_params=_cparams(1),
        name="dest",
    )(idx_t, rank_t, lstart_col)


TOK_UNROLL = 4


def _disp_kernel(dest_ref, cnt_ref, gst_ref, pads_ref, padn_ref, nused_ref, u_ref, xg_ref,
                 xloc, zbuf, sem):
    g_tok = u_ref.shape[0]
    n_rows = g_tok * TOP_K
    g = pl.program_id(0)
    last = pl.num_programs(0) - 1
    slot = g % 2

    def drain(s):
        pltpu.make_async_copy(xloc.at[s], xg_ref.at[pl.ds(0, n_rows)], sem.at[s]).wait()

    @pl.when(g == 0)
    def _():
        zbuf[...] = jnp.zeros_like(zbuf)

    @pl.when(g >= 2)
    def _():
        drain(slot)

    xs = xloc.at[slot]
    xs2 = xloc.reshape(2, n_rows * SUBLANES, LANES).at[slot]

    def fill(i, carry):
        for uu in range(TOK_UNROLL):
            t = i * TOK_UNROLL + uu
            row = u_ref[t]
            for k in range(TOP_K):
                xs2[pl.ds(pl.multiple_of(dest_ref[t * TOP_K + k], SUBLANES), SUBLANES), :] = row
        return carry

    lax.fori_loop(0, g_tok // TOK_UNROLL, fill, 0)

    def seg(e, ls):
        c = cnt_ref[0, e]

        @pl.when(c > 0)
        def _():
            pltpu.make_async_copy(xs.at[pl.ds(ls, c)], xg_ref.at[pl.ds(gst_ref[0, e], c)],
                                  sem.at[slot]).start()

        return ls + c

    lax.fori_loop(0, N_EXPERTS, seg, 0)

    @pl.when(g == last)
    def _():
        def pad_copy(e):
            n = padn_ref[0, e]
            return pltpu.make_async_copy(zbuf.at[pl.ds(0, n)], xg_ref.at[pl.ds(pads_ref[0, e], n)],
                                         sem.at[2])

        def pad_start(e, carry):
            @pl.when(padn_ref[0, e] > 0)
            def _():
                pad_copy(e).start()
            return carry

        def pad_wait(e, carry):
            @pl.when(padn_ref[0, e] > 0)
            def _():
                pad_copy(e).wait()
            return carry

        n_blocks = xg_ref.shape[0] // R_EXP

        def tail_copy(i):
            return pltpu.make_async_copy(
                zbuf, xg_ref.at[pl.ds(pl.multiple_of(i * R_EXP, R_EXP), R_EXP)], sem.at[2])

        def tail_start(i, carry):
            tail_copy(i).start()
            return carry

        def tail_wait(i, carry):
            tail_copy(i).wait()
            return carry

        lax.fori_loop(0, N_EXPERTS, pad_start, 0)
        lax.fori_loop(nused_ref[0, 0], n_blocks, tail_start, 0)
        lax.fori_loop(0, N_EXPERTS, pad_wait, 0)
        lax.fori_loop(nused_ref[0, 0], n_blocks, tail_wait, 0)
        drain(1 - slot)
        drain(slot)


def _dispatch(dest_blk, cnt_tab, gst_tab, pad_start, pad_len, nused, urows, p_rows):
    t = urows.shape[0]
    n_groups = t // G_TOK
    assert n_groups >= 2
    smem_grp = lambda: pl.BlockSpec((None, 1, N_EXPERTS), lambda i: (i, 0, 0),
                                    memory_space=pltpu.SMEM)
    smem_all = lambda: pl.BlockSpec((1, N_EXPERTS), lambda i: (0, 0), memory_space=pltpu.SMEM)
    return pl.pallas_call(
        _disp_kernel,
        grid=(n_groups,),
        in_specs=[
            pl.BlockSpec((G_TOK * TOP_K,), lambda i: (i,), memory_space=pltpu.SMEM),
            smem_grp(), smem_grp(), smem_all(), smem_all(),
            pl.BlockSpec((1, 1), lambda i: (0, 0), memory_space=pltpu.SMEM),
            pl.BlockSpec((G_TOK, SUBLANES, LANES), lambda i: (i, 0, 0)),
        ],
        out_specs=pl.BlockSpec(memory_space=pl.ANY),
        out_shape=jax.ShapeDtypeStruct((p_rows, SUBLANES, LANES), f32),
        scratch_shapes=[pltpu.VMEM((2, G_TOK * TOP_K, SUBLANES, LANES), f32),
                        pltpu.VMEM((R_EXP, SUBLANES, LANES), f32),
                        pltpu.SemaphoreType.DMA((3,))],
        compiler_params=_cparams(1),
        name="dispatch",
    )(dest_blk, cnt_tab, gst_tab, pad_start, pad_len, nused.reshape(1, 1), urows)


def _experts_kernel(be_ref, nused_ref, x_ref, wgu_ref, wd_ref, y_ref, wgu_b, wd_b):
    rows = x_ref.shape[0] // SUBLANES
    d = wgu_ref.shape[0]
    i = pl.program_id(0)

    @pl.when((i == 0) | (be_ref[i] != be_ref[jnp.maximum(i - 1, 0)]))
    def _():
        wgu_b[...] = wgu_ref[...].astype(bf16)
        wd_b[...] = wd_ref[...].astype(bf16)

    @pl.when(i < nused_ref[0])
    def _():
        sub = rows // EXP_SPLIT
        for hf in range(EXP_SPLIT):
            base = hf * sub * SUBLANES
            x = jnp.concatenate(
                [x_ref[pl.ds(base + j, sub, stride=SUBLANES), :] for j in range(d // LANES)],
                axis=1).astype(bf16)
            gu = _dot(x, wgu_b[...])
            de = wd_b.shape[0]
            act = (jax.nn.silu(gu[:, :de]) * gu[:, de:]).astype(bf16)
            y = _dot(act, wd_b[...])
            for j in range(d // LANES):
                y_ref[pl.ds(base + j, sub, stride=SUBLANES), :] = y[:, j * LANES:(j + 1) * LANES]

    @pl.when(i >= nused_ref[0])
    def _():
        y_ref[...] = jnp.zeros_like(y_ref)


def _experts(block_exp, nused, xg2, w_gu, w_down, layer):
    p8 = xg2.shape[0]
    _, n_e, d, de2 = w_gu.shape
    rb = R_EXP * SUBLANES
    nb = p8 // rb
    grid_spec = pltpu.PrefetchScalarGridSpec(
        num_scalar_prefetch=2,
        grid=(nb,),
        in_specs=[
            pl.BlockSpec((rb, LANES), lambda i, be, nu: (jnp.minimum(i, nu[0] - 1), 0)),
            pl.BlockSpec((None, None, d, de2), lambda i, be, nu: (layer, be[i], 0, 0)),
            pl.BlockSpec((None, None, de2 // 2, d), lambda i, be, nu: (layer, be[i], 0, 0)),
        ],
        out_specs=pl.BlockSpec((rb, LANES), lambda i, be, nu: (i, 0)),
        scratch_shapes=[pltpu.VMEM((d, de2), bf16), pltpu.VMEM((de2 // 2, d), bf16)],
    )
    return pl.pallas_call(
        _experts_kernel,
        grid_spec=grid_spec,
        out_shape=jax.ShapeDtypeStruct((p8, LANES), f32),
        compiler_params=_cparams(1),
        name="experts",
    )(block_exp, nused, xg2, w_gu, w_down)


def _comb_kernel(dest_ref, wts_ref, cnt_ref, gst_ref, cntn_ref, gstn_ref, hmid_ref, mod_ref, fg_ref,
                 yg_ref, o_ref, yloc, acc, sem, *, final):
    g_tok, d = hmid_ref.shape
    n_rows = g_tok * TOP_K
    g = pl.program_id(0)
    n_groups = pl.num_programs(0)
    slot = g % 2

    def fetch(cnt_r, gst_r, s):
        def seg(e, ls):
            c = cnt_r[0, e]

            @pl.when(c > 0)
            def _():
                pltpu.make_async_copy(yg_ref.at[pl.ds(gst_r[0, e], c)],
                                      yloc.at[s, pl.ds(ls, c)], sem.at[s]).start()

            return ls + c

        lax.fori_loop(0, N_EXPERTS, seg, 0)

    @pl.when(g == 0)
    def _():
        fetch(cnt_ref, gst_ref, 0)

    @pl.when(g + 1 < n_groups)
    def _():
        fetch(cntn_ref, gstn_ref, 1 - slot)

    pltpu.make_async_copy(yg_ref.at[pl.ds(0, n_rows)], yloc.at[slot], sem.at[slot]).wait()

    ys2 = yloc.reshape(2, n_rows * SUBLANES, LANES).at[slot]

    def reduce(i, carry):
        for uu in range(TOK_UNROLL):
            t = i * TOK_UNROLL + uu
            a = None
            for k in range(TOP_K):
                off = pl.multiple_of(dest_ref[t * TOP_K + k], SUBLANES)
                term = ys2[pl.ds(off, SUBLANES), :] * wts_ref[t * TOP_K + k]
                a = term if a is None else a + term
            acc[pl.ds(pl.multiple_of(t * SUBLANES, SUBLANES), SUBLANES), :] = a
        return carry

    lax.fori_loop(0, g_tok // TOK_UNROLL, reduce, 0)
    for j in range(d // LANES):
        cs = slice(j * LANES, (j + 1) * LANES)
        o_ref[:, cs] = hmid_ref[:, cs] + mod_ref[:, 5 * d + j * LANES:5 * d + (j + 1) * LANES] * \
            acc[pl.ds(j, g_tok, stride=SUBLANES), :]
    if final:
        o_ref[...] = _rmsnorm_rows(o_ref[...], fg_ref[...])


def _combine(dest_blk, wts_blk, cnt_tab, gst_tab, hmid, mod, final_g, yg, final):
    t, d = hmid.shape
    n_groups = t // G_TOK
    gpb = n_groups // mod.shape[0]
    row = pl.BlockSpec((G_TOK, d), lambda i: (i, 0))
    smem_tok = lambda: pl.BlockSpec((G_TOK * TOP_K,), lambda i: (i,), memory_space=pltpu.SMEM)
    smem_cur = lambda: pl.BlockSpec((None, 1, N_EXPERTS), lambda i: (i, 0, 0),
                                    memory_space=pltpu.SMEM)
    smem_nxt = lambda: pl.BlockSpec((None, 1, N_EXPERTS),
                                    lambda i: (jnp.minimum(i + 1, n_groups - 1), 0, 0),
                                    memory_space=pltpu.SMEM)
    return pl.pallas_call(
        functools.partial(_comb_kernel, final=final),
        grid=(n_groups,),
        in_specs=[smem_tok(), smem_tok(), smem_cur(), smem_cur(), smem_nxt(), smem_nxt(), row,
                  pl.BlockSpec((None, 1, mod.shape[-1]), lambda i: (i // gpb, 0, 0)),
                  pl.BlockSpec(final_g.shape, lambda i: (0, 0)),
                  pl.BlockSpec(memory_space=pl.ANY)],
        out_specs=row,
        out_shape=jax.ShapeDtypeStruct((t, d), f32),
        scratch_shapes=[pltpu.VMEM((2, G_TOK * TOP_K, SUBLANES, LANES), f32),
                        pltpu.VMEM((G_TOK * SUBLANES, LANES), f32),
                        pltpu.SemaphoreType.DMA((2,))],
        compiler_params=_cparams(1),
        name="combine_final" if final else "combine",
    )(dest_blk, wts_blk, cnt_tab, gst_tab, cnt_tab, gst_tab, hmid, mod, final_g, yg)


def _tok_major(a_t):
    return a_t.T.reshape(-1)


def kernel(x, c, ada_w, ada_b, norm1_g, w_in, conv_w, conv_b, rg_wa, rg_ba, rg_wx, rg_bx, rg_lambda,
           gla_w_up, gla_b_up, gla_norm_g, w_branch_a, w_branch_b, w_out, norm2_g, router_w, router_b,
           exp_w_gu, exp_w_down, shared_w_gu, shared_w_down, final_g):
    bsz, s, d = x.shape
    n_layers = ada_w.shape[0]
    t = bsz * s
    dqk = GLA_HEADS * GLA_DK
    n_assign = t * TOP_K
    nb = -(-(n_assign + N_EXPERTS * (R_EXP - 1)) // R_EXP)
    p = nb * R_EXP
    n_groups = t // G_TOK

    mod_all = _ada(c, ada_w, ada_b)
    h = x
    for l in range(n_layers):
        mod = mod_all[l].reshape(bsz, 1, 6 * d)
        wl = w_in[l]
        lr0 = 2 * d + 2 * dqk + 2 * d
        w_main = jnp.concatenate([wl[:, :lr0], wl[:, lr0 + GLA_RANK:]], axis=1).astype(bf16)
        w_lr = jnp.pad(wl[:, lr0:lr0 + GLA_RANK], ((0, 0), (0, LANES - GLA_RANK))).astype(bf16)
        w_up = jnp.pad(gla_w_up[l], ((0, LANES - GLA_RANK), (0, 0))).astype(bf16)
        xr, gg, q, k, v, og, ga, gb, la = _in_proj(
            h, mod, norm1_g[l].reshape(1, d), w_main, w_lr, w_up, gla_b_up[l].reshape(1, dqk))

        wab = jnp.concatenate([rg_wa[l], rg_wx[l]], axis=-1).astype(bf16)
        ya_in = _rglru(xr, gg, conv_w[l], conv_b[l].reshape(1, d), wab,
                       rg_ba[l].reshape(1, d), rg_bx[l].reshape(1, d), rg_lambda[l].reshape(1, d))
        ob = _gla(q, k, v, la, og, gla_norm_g[l].reshape(1, d))

        wr_t = router_w[l].T
        wr_hi = wr_t.astype(bf16)
        wr_lo = (wr_t - wr_hi.astype(f32)).astype(bf16)
        hmid, urows, idx_t, wts_t, rank_t, counts = _post(
            ya_in, ob, ga, gb, h, mod,
            w_branch_a[l].astype(bf16), w_branch_b[l].astype(bf16), w_out[l].astype(bf16),
            norm2_g[l].reshape(1, d), shared_w_gu[l].astype(bf16), shared_w_down[l].astype(bf16),
            wr_hi, wr_lo, router_b[l].reshape(N_EXPERTS, 1))

        cnt = counts[:, :, 0].astype(jnp.int32)
        lstart = jnp.cumsum(cnt, axis=1) - cnt
        ctot = jnp.sum(cnt, axis=0)
        padded = (ctot + R_EXP - 1) // R_EXP * R_EXP
        ends = jnp.cumsum(padded)
        pstart = ends - padded
        gstart = pstart[None, :] + jnp.cumsum(cnt, axis=0) - cnt
        blk_row = jnp.arange(nb, dtype=jnp.int32) * R_EXP
        block_exp = jnp.minimum(
            jnp.sum((ends[None, :] <= blk_row[:, None]).astype(jnp.int32), axis=1),
            N_EXPERTS - 1).astype(jnp.int32)
        nused = (ends[-1:] // R_EXP).astype(jnp.int32)
        cnt_tab = cnt.reshape(n_groups, 1, N_EXPERTS)
        gst_tab = gstart.astype(jnp.int32).reshape(n_groups, 1, N_EXPERTS)

        dest_t = _dest(idx_t, rank_t, lstart.astype(f32).reshape(n_groups, N_EXPERTS, 1))
        dest_blk = _tok_major(dest_t)
        xg = _dispatch(dest_blk, cnt_tab, gst_tab,
                       (pstart + ctot).astype(jnp.int32).reshape(1, N_EXPERTS),
                       (padded - ctot).astype(jnp.int32).reshape(1, N_EXPERTS),
                       nused, urows.reshape(t, SUBLANES, LANES), p)
        yg = _experts(block_exp, nused, xg.reshape(p * SUBLANES, LANES), exp_w_gu, exp_w_down, l)
        h = _combine(dest_blk, _tok_major(wts_t), cnt_tab, gst_tab, hmid.reshape(t, d), mod,
                     final_g.reshape(1, d), yg.reshape(p, SUBLANES, LANES),
                     l == n_layers - 1).reshape(bsz, s, d)
    return h
```

```python
import functools

import jax
import jax.numpy as jnp
import numpy as np
from jax import lax
from jax.experimental import pallas as pl
from jax.experimental.pallas import tpu as pltpu

EPS = 1e-6
CHUNK = 64
CONV_W = 4
RG_C = 8.0
RNN_BLOCKS = 8
GLA_HEADS = 4
GLA_DK = 128
GLA_DV = 256
GLA_RANK = 16
GLA_TAU = 16.0
N_EXPERTS = 128
TOP_K = 8
D_EXPERT = 256
ROUTED_SCALE = 2.5

LANES = 128
SUBLANES = 8
PACKED_SUBLANES = 4
VMEM_LIMIT = 56 * 1024 * 1024

TM_IN = 512
TS_RNN = 256
TS_GLA = 256
G_TOK = 512
POST_SPLIT = 2
R_EXP = 512
EXP_SPLIT = 1

bf16 = jnp.bfloat16
f32 = jnp.float32


def _cparams(n_axes):
    return pltpu.CompilerParams(
        dimension_semantics=("arbitrary",) * n_axes,
        vmem_limit_bytes=VMEM_LIMIT,
    )


def _dot(a, b):
    return jnp.dot(a, b, preferred_element_type=f32)


def _dot_nt(a, b):
    return lax.dot_general(a, b, (((1,), (1,)), ((), ())), preferred_element_type=f32)


def _dot_tn(a, b):
    return lax.dot_general(a, b, (((0,), (0,)), ((), ())), preferred_element_type=f32)


def _split_bf16(x):
    hi = x.astype(bf16)
    lo = (x - hi.astype(f32)).astype(bf16)
    return hi, lo


def _pack_bf16_pair(lo, hi):
    lo_bits = lax.bitcast_convert_type(lo.astype(f32), jnp.uint32) >> 16
    hi_bits = lax.bitcast_convert_type(hi.astype(f32), jnp.uint32) & jnp.uint32(0xFFFF0000)
    return lax.bitcast_convert_type(hi_bits | lo_bits, jnp.int32)


def _unpack_bf16_pair(word):
    bits = lax.bitcast_convert_type(word, jnp.uint32)
    lo = lax.bitcast_convert_type(bits << 16, f32).astype(bf16)
    hi = lax.bitcast_convert_type(bits & jnp.uint32(0xFFFF0000), f32).astype(bf16)
    return lo, hi


def _packed_rows_view(buf):
    two, n, sl, ln = buf.shape
    return (buf.reshape(two, n // 2, 2 * sl, ln).bitcast(jnp.int32)
            .reshape(two, n * PACKED_SUBLANES, ln))


def _rmsnorm_rows(x, g):
    return x * lax.rsqrt(jnp.mean(x * x, axis=-1, keepdims=True) + EPS) * g


def _log_sigmoid(x):
    return jnp.minimum(x, 0.0) - jnp.log1p(jnp.exp(-jnp.abs(x)))


def _softplus(x):
    return jnp.maximum(x, 0.0) + jnp.log1p(jnp.exp(-jnp.abs(x)))


def _ada_kernel(c_ref, w_ref, b_ref, o_ref):
    c_act = jax.nn.silu(c_ref[...])
    c_hi, c_lo = _split_bf16(c_act)
    w_hi, w_lo = _split_bf16(w_ref[...])
    acc = _dot(c_hi, w_hi) + _dot(c_lo, w_hi) + _dot(c_hi, w_lo)
    o_ref[...] = acc + b_ref[...]


def _ada(c, ada_w, ada_b):
    n_layers, d, n6 = ada_w.shape
    bsz = c.shape[0]
    tn = 512
    return pl.pallas_call(
        _ada_kernel,
        grid=(n_layers, n6 // tn),
        in_specs=[
            pl.BlockSpec((bsz, d), lambda l, j: (0, 0)),
            pl.BlockSpec((None, d, tn), lambda l, j: (l, 0, j)),
            pl.BlockSpec((None, 1, tn), lambda l, j: (l, 0, j)),
        ],
        out_specs=pl.BlockSpec((None, bsz, tn), lambda l, j: (l, 0, j)),
        out_shape=jax.ShapeDtypeStruct((n_layers, bsz, n6), f32),
        compiler_params=_cparams(2),
        name="ada_mod",
    )(c, ada_w, ada_b.reshape(n_layers, 1, n6))


def _in_kernel(h_ref, mod_ref, g_ref, w_ref, wlr_ref, wup_ref, bup_ref,
               xr_ref, gg_ref, q_ref, k_ref, v_ref, og_ref, ga_ref, gb_ref, la_ref, u_scr):
    d = h_ref.shape[-1]
    sh1 = mod_ref[:, 0:d]
    sc1 = mod_ref[:, d:2 * d]
    u = _rmsnorm_rows(h_ref[...], g_ref[...]) * (1.0 + sc1) + sh1
    u_scr[...] = u.astype(bf16)

    def proj(c0, width):
        return _dot(u_scr[...], w_ref[:, c0:c0 + width])

    cw = 512
    for c in range(d // cw):
        xr_ref[:, c * cw:(c + 1) * cw] = proj(c * cw, cw)
    for c in range(d // cw):
        y = proj(d + c * cw, cw)
        gg_ref[:, c * cw:(c + 1) * cw] = jax.nn.gelu(y, approximate=True).astype(bf16)
    dqk = GLA_HEADS * GLA_DK
    q_ref[...] = (proj(2 * d, dqk) * (GLA_DK ** -0.5)).astype(bf16)
    k_ref[...] = proj(2 * d + dqk, dqk).astype(bf16)
    base = 2 * d + 2 * dqk
    for c in range(d // cw):
        v_ref[:, c * cw:(c + 1) * cw] = proj(base + c * cw, cw).astype(bf16)
    base += d
    for c in range(d // cw):
        y = proj(base + c * cw, cw)
        og_ref[:, c * cw:(c + 1) * cw] = jax.nn.silu(y).astype(bf16)
    base += d
    for c in range(d // cw):
        y = proj(base + c * cw, cw)
        ga_ref[:, c * cw:(c + 1) * cw] = jax.nn.sigmoid(y).astype(bf16)
    base += d
    for c in range(d // cw):
        y = proj(base + c * cw, cw)
        gb_ref[:, c * cw:(c + 1) * cw] = jax.nn.sigmoid(y).astype(bf16)
    lr = _dot(u_scr[...], wlr_ref[...]).astype(bf16)
    logit = _dot(lr, wup_ref[...]) + bup_ref[...]
    la_ref[...] = _log_sigmoid(logit) * (1.0 / GLA_TAU)


def _in_proj(h, mod, norm_g, w_main, w_lr, w_up, b_up):
    bsz, s, d = h.shape
    tm = TM_IN
    dqk = GLA_HEADS * GLA_DK
    row = lambda width: pl.BlockSpec((None, tm, width), lambda b, i: (b, i, 0))
    full = lambda a: pl.BlockSpec(a.shape, lambda b, i: (0,) * a.ndim)
    out_shapes = [
        jax.ShapeDtypeStruct((bsz, s, d), f32),
        jax.ShapeDtypeStruct((bsz, s, d), bf16),
        jax.ShapeDtypeStruct((bsz, s, dqk), bf16),
        jax.ShapeDtypeStruct((bsz, s, dqk), bf16),
        jax.ShapeDtypeStruct((bsz, s, d), bf16),
        jax.ShapeDtypeStruct((bsz, s, d), bf16),
        jax.ShapeDtypeStruct((bsz, s, d), bf16),
        jax.ShapeDtypeStruct((bsz, s, d), bf16),
        jax.ShapeDtypeStruct((bsz, s, dqk), f32),
    ]
    return pl.pallas_call(
        _in_kernel,
        grid=(bsz, s // tm),
        in_specs=[
            row(d),
            pl.BlockSpec((None, 1, mod.shape[-1]), lambda b, i: (b, 0, 0)),
            full(norm_g),
            pl.BlockSpec(w_main.shape, lambda b, i: (0, 0), pipeline_mode=pl.Buffered(1)),
            full(w_lr), full(w_up), full(b_up),
        ],
        out_specs=[row(d), row(d), row(dqk), row(dqk), row(d), row(d), row(d), row(d), row(dqk)],
        out_shape=out_shapes,
        scratch_shapes=[pltpu.VMEM((tm, d), bf16)],
        compiler_params=_cparams(2),
        name="in_proj",
    )(h, mod, norm_g, w_main, w_lr, w_up, b_up)


def _rglru_kernel(xr_ref, gg_ref, cw_ref, cb_ref, wab_ref, ba_ref, bx_ref, lam_ref,
                  o_ref, xpad, hcar):
    ts, d = xr_ref.shape
    si = pl.program_id(1)

    @pl.when(si == 0)
    def _():
        xpad[0:SUBLANES, :] = jnp.zeros((SUBLANES, d), f32)
        hcar[...] = jnp.zeros_like(hcar)

    xpad[SUBLANES:SUBLANES + ts, :] = xr_ref[...]
    xc = cb_ref[...] + xpad[SUBLANES:SUBLANES + ts, :] * cw_ref[CONV_W - 1:CONV_W, :]
    for j in range(CONV_W - 1):
        off = SUBLANES - (CONV_W - 1) + j
        xc = xc + xpad[off:off + ts, :] * cw_ref[j:j + 1, :]
    xpad[0:SUBLANES, :] = xpad[ts:ts + SUBLANES, :]

    groups = ts // SUBLANES
    bw = d // RNN_BLOCKS
    sub = lax.broadcasted_iota(jnp.int32, (groups, SUBLANES, bw), 1)
    first_row = (lax.broadcasted_iota(jnp.int32, (ts, bw), 0) == 0) & (si == 0)
    sp = _softplus(-lam_ref[...])
    xcb = xc.astype(bf16)
    for hb in range(RNN_BLOCKS):
        cs = slice(hb * bw, (hb + 1) * bw)
        x_h = xc[:, cs]
        g = _dot(xcb[:, cs], wab_ref[hb])
        r = jax.nn.sigmoid(g[:, :bw] + ba_ref[:, cs])
        i = jax.nn.sigmoid(g[:, bw:] + bx_ref[:, cs])
        log_a = (-RG_C) * r * sp[:, cs]
        a = jnp.exp(log_a)
        mult = jnp.sqrt(1.0 - a * a)
        mult = jnp.where(first_row, 1.0, mult)
        bt = mult * (i * x_h)
        av = a.reshape(groups, SUBLANES, bw)
        bv = bt.reshape(groups, SUBLANES, bw)
        for sh in (1, 2, 4):
            a_prev = pltpu.roll(av, sh, 1)
            b_prev = pltpu.roll(bv, sh, 1)
            keep = sub >= sh
            bv = jnp.where(keep, av * b_prev + bv, bv)
            av = jnp.where(keep, av * a_prev, av)
        hin = hcar[0:1, cs]
        outs = []
        for gi in range(groups):
            hg = av[gi] * hin + bv[gi]
            outs.append(hg)
            hin = hg[SUBLANES - 1:SUBLANES, :]
        hcar[0:1, cs] = hin
        hs = jnp.concatenate(outs, axis=0)
        o_ref[:, cs] = (hs * gg_ref[:, cs].astype(f32)).astype(bf16)


def _rglru(xr, gg, conv_w, conv_b, wab, ba, bx, lam):
    bsz, s, d = xr.shape
    ts = TS_RNN
    row = pl.BlockSpec((None, ts, d), lambda b, i: (b, i, 0))
    full = lambda a: pl.BlockSpec(a.shape, lambda b, i: (0,) * a.ndim)
    return pl.pallas_call(
        _rglru_kernel,
        grid=(bsz, s // ts),
        in_specs=[row, row, full(conv_w), full(conv_b), full(wab), full(ba), full(bx), full(lam)],
        out_specs=row,
        out_shape=jax.ShapeDtypeStruct((bsz, s, d), bf16),
        scratch_shapes=[pltpu.VMEM((ts + SUBLANES, d), f32), pltpu.VMEM((SUBLANES, d), f32)],
        compiler_params=_cparams(2),
        name="rglru",
    )(xr, gg, conv_w, conv_b, wab, ba, bx, lam)


def _gla_kernel(q_ref, k_ref, v_ref, la_ref, og_ref, g_ref, o_ref, st):
    ts = q_ref.shape[0]
    si = pl.program_id(1)

    @pl.when(si == 0)
    def _():
        st[...] = jnp.zeros_like(st)

    ri = lax.broadcasted_iota(jnp.int32, (CHUNK, CHUNK), 0)
    ci = lax.broadcasted_iota(jnp.int32, (CHUNK, CHUNK), 1)
    tri = jnp.where(ci <= ri, 1.0, 0.0).astype(bf16)
    for c in range(ts // CHUNK):
        rs = slice(c * CHUNK, (c + 1) * CHUNK)
        la_hi, la_lo = _split_bf16(la_ref[rs, :])
        cum = _dot(tri, la_hi) + _dot(tri, la_lo)
        tot = cum[CHUNK - 1:CHUNK, :]
        kdec = (k_ref[rs, :].astype(f32) * jnp.exp(tot - cum)).astype(bf16)
        dec = jnp.exp(tot)
        for hd in range(GLA_HEADS):
            ks = slice(hd * GLA_DK, (hd + 1) * GLA_DK)
            vs = slice(hd * GLA_DV, (hd + 1) * GLA_DV)
            s_new = st[hd] * dec[:, ks] + _dot_tn(v_ref[rs, vs], kdec[:, ks])
            st[hd] = s_new
            o = _dot_nt(q_ref[rs, ks], s_new.astype(bf16))
            o = _rmsnorm_rows(o, g_ref[:, vs])
            o_ref[rs, vs] = (o * og_ref[rs, vs].astype(f32)).astype(bf16)


def _gla(q, k, v, la, og, norm_g):
    bsz, s, dqk = q.shape
    d = v.shape[-1]
    ts = TS_GLA
    row = lambda width: pl.BlockSpec((None, ts, width), lambda b, i: (b, i, 0))
    return pl.pallas_call(
        _gla_kernel,
        grid=(bsz, s // ts),
        in_specs=[row(dqk), row(dqk), row(d), row(dqk), row(d),
                  pl.BlockSpec(norm_g.shape, lambda b, i: (0, 0))],
        out_specs=row(d),
        out_shape=jax.ShapeDtypeStruct((bsz, s, d), bf16),
        scratch_shapes=[pltpu.VMEM((GLA_HEADS, GLA_DV, GLA_DK), f32)],
        compiler_params=_cparams(2),
        name="gla",
    )(q, k, v, la, og, norm_g)


def _post_kernel(ya_ref, ob_ref, ga_ref, gb_ref, h_ref, mod_ref, wa_ref, wb_ref, wo_ref, g2_ref,
                 wsgu_ref, wsd_ref, wrh_ref, wrl_ref, rb_ref,
                 hmid_ref, urow_ref, idx_ref, wts_ref, rank_ref, cnt_ref):
    tm, d = h_ref.shape
    sub = tm // POST_SPLIT
    g1 = mod_ref[:, 2 * d:3 * d]
    sh2 = mod_ref[:, 3 * d:4 * d]
    sc2 = mod_ref[:, 4 * d:5 * d]
    g2 = mod_ref[:, 5 * d:6 * d]
    n_e = wrh_ref.shape[1]
    e_iota = lax.broadcasted_iota(jnp.int32, (n_e, sub), 0).astype(f32)
    ri = lax.broadcasted_iota(jnp.int32, (sub, sub), 0)
    ci = lax.broadcasted_iota(jnp.int32, (sub, sub), 1)
    upper = jnp.where(ri <= ci, 1.0, 0.0).astype(bf16)
    running = jnp.zeros((n_e, LANES), f32)
    for hf in range(POST_SPLIT):
        rs = slice(hf * sub, (hf + 1) * sub)
        ya = _dot(ya_ref[rs, :], wa_ref[...])
        yb = _dot(ob_ref[rs, :], wb_ref[...])
        m = (ga_ref[rs, :].astype(f32) * ya + gb_ref[rs, :].astype(f32) * yb).astype(bf16)
        h1 = h_ref[rs, :] + g1 * _dot(m, wo_ref[...])
        u2 = _rmsnorm_rows(h1, g2_ref[...]) * (1.0 + sc2) + sh2
        u_hi, u_lo = _split_bf16(u2)

        gu = _dot(u_hi, wsgu_ref[...])
        dsh = wsd_ref.shape[0]
        act = (jax.nn.silu(gu[:, :dsh]) * gu[:, dsh:]).astype(bf16)
        hmid_ref[rs, :] = h1 + g2 * _dot(act, wsd_ref[...])

        uw = urow_ref.bitcast(jnp.int32)
        for jp in range(d // LANES // 2):
            word = _pack_bf16_pair(u_hi[:, (2 * jp) * LANES:(2 * jp + 1) * LANES],
                                   u_hi[:, (2 * jp + 1) * LANES:(2 * jp + 2) * LANES])
            uw[pl.ds(hf * sub * PACKED_SUBLANES + jp, sub, stride=PACKED_SUBLANES), :] = word

        logits = (_dot(u_hi, wrh_ref[...]) + _dot(u_lo, wrh_ref[...])
                  + _dot(u_hi, wrl_ref[...])).T
        scores = jax.nn.sigmoid(logits)
        work = scores + rb_ref[...]
        onehots, idxs, svals = [], [], []
        for _ in range(TOP_K):
            mx = jnp.max(work, axis=0, keepdims=True)
            ik = jnp.min(jnp.where(work == mx, e_iota, float(n_e)), axis=0, keepdims=True)
            oh = e_iota == ik
            onehots.append(oh)
            idxs.append(ik.astype(jnp.int32))
            svals.append(jnp.sum(jnp.where(oh, scores, 0.0), axis=0, keepdims=True))
            work = jnp.where(oh, -jnp.inf, work)
        ssum = svals[0]
        for sv in svals[1:]:
            ssum = ssum + sv
        idx_ref[:, rs] = jnp.concatenate(idxs, axis=0)
        wts_ref[:, rs] = jnp.concatenate([sv / ssum * ROUTED_SCALE for sv in svals], axis=0)

        member = jnp.where(onehots[0], 1.0, 0.0)
        for oh in onehots[1:]:
            member = member + jnp.where(oh, 1.0, 0.0)
        incl = _dot(member.astype(bf16), upper)
        base = jnp.concatenate([running] * (sub // LANES), axis=1)
        pos = base + incl - member
        ranks = [jnp.sum(jnp.where(oh, pos, 0.0), axis=0, keepdims=True) for oh in onehots]
        rank_ref[:, rs] = jnp.concatenate(ranks, axis=0).astype(jnp.int32)
        running = running + jnp.broadcast_to(incl[:, sub - 1:sub], running.shape)
    cnt_ref[...] = running


def _post(ya_in, ob, ga, gb, h, mod, wa, wb, wo, g2, wsgu, wsd, wr_hi, wr_lo, rb):
    bsz, s, d = h.shape
    tm = G_TOK
    t = bsz * s
    nsb = s // tm
    n_groups = t // G_TOK
    row = pl.BlockSpec((None, tm, d), lambda b, i: (b, i, 0))
    full = lambda a: pl.BlockSpec(a.shape, lambda b, i: (0,) * a.ndim)
    tok = pl.BlockSpec((TOP_K, tm), lambda b, i: (0, b * nsb + i))
    return pl.pallas_call(
        _post_kernel,
        grid=(bsz, nsb),
        in_specs=[row, row, row, row, row,
                  pl.BlockSpec((None, 1, mod.shape[-1]), lambda b, i: (b, 0, 0)),
                  full(wa), full(wb), full(wo), full(g2), full(wsgu), full(wsd),
                  full(wr_hi), full(wr_lo), full(rb)],
        out_specs=[row,
                   pl.BlockSpec((tm * SUBLANES, LANES), lambda b, i: (b * nsb + i, 0)),
                   tok, tok, tok,
                   pl.BlockSpec((None, N_EXPERTS, LANES), lambda b, i: (b * nsb + i, 0, 0))],
        out_shape=[jax.ShapeDtypeStruct((bsz, s, d), f32),
                   jax.ShapeDtypeStruct((t * SUBLANES, LANES), bf16),
                   jax.ShapeDtypeStruct((TOP_K, t), jnp.int32),
                   jax.ShapeDtypeStruct((TOP_K, t), f32),
                   jax.ShapeDtypeStruct((TOP_K, t), jnp.int32),
                   jax.ShapeDtypeStruct((n_groups, N_EXPERTS, LANES), f32)],
        compiler_params=_cparams(2),
        name="post",
    )(ya_in, ob, ga, gb, h, mod, wa, wb, wo, g2, wsgu, wsd, wr_hi, wr_lo, rb)


def _dest_kernel(idx_ref, rank_ref, pstart_ref, o_ref):
    tn = idx_ref.shape[1]
    n_e = pstart_ref.shape[0]
    e_iota = lax.broadcasted_iota(jnp.int32, (n_e, tn), 0)
    rows = []
    for k in range(TOP_K):
        start = jnp.sum(jnp.where(e_iota == idx_ref[k:k + 1, :], pstart_ref[...], 0.0),
                        axis=0, keepdims=True)
        rows.append(start.astype(jnp.int32) + rank_ref[k:k + 1, :])
    o_ref[...] = jnp.concatenate(rows, axis=0)


def _dest(idx_t, rank_t, lstart_col):
    t = idx_t.shape[1]
    tn = G_TOK
    tok = pl.BlockSpec((TOP_K, tn), lambda i: (0, i))
    return pl.pallas_call(
        _dest_kernel,
        grid=(t // tn,),
        in_specs=[tok, tok, pl.BlockSpec((None, N_EXPERTS, 1), lambda i: (i, 0, 0))],
        out_specs=tok,
        out_shape=jax.ShapeDtypeStruct((TOP_K, t), jnp.int32),
        compiler_params=_cparams(1),
        name="dest",
    )(idx_t, rank_t, lstart_col)


TOK_UNROLL = 4


def _disp_kernel(dest_ref, cnt_ref, gst_ref, pads_ref, padn_ref, nused_ref, u_ref, xg_ref,
                 xloc, zbuf, sem):
    g_tok = u_ref.shape[0]
    n_rows = g_tok * TOP_K
    g = pl.program_id(0)
    last = pl.num_programs(0) - 1
    slot = g % 2

    def drain(s):
        pltpu.make_async_copy(xloc.at[s], xg_ref.at[pl.ds(0, n_rows)], sem.at[s]).wait()

    @pl.when(g == 0)
    def _():
        zbuf[...] = jnp.zeros_like(zbuf)

    @pl.when(g >= 2)
    def _():
        drain(slot)

    xs = xloc.at[slot]
    uw = u_ref.bitcast(jnp.int32)
    xw2 = _packed_rows_view(xloc).at[slot]

    def fill(i, carry):
        for uu in range(TOK_UNROLL):
            t = i * TOK_UNROLL + uu
            row = uw[t]
            for k in range(TOP_K):
                off = pl.multiple_of(dest_ref[t * TOP_K + k] * PACKED_SUBLANES, PACKED_SUBLANES)
                xw2[pl.ds(off, PACKED_SUBLANES), :] = row
        return carry

    lax.fori_loop(0, g_tok // TOK_UNROLL, fill, 0)

    def seg(e, ls):
        c = cnt_ref[0, e]

        @pl.when(c > 0)
        def _():
            pltpu.make_async_copy(xs.at[pl.ds(ls, c)], xg_ref.at[pl.ds(gst_ref[0, e], c)],
                                  sem.at[slot]).start()

        return ls + c

    lax.fori_loop(0, N_EXPERTS, seg, 0)

    @pl.when(g == last)
    def _():
        def pad_copy(e):
            n = padn_ref[0, e]
            return pltpu.make_async_copy(zbuf.at[pl.ds(0, n)], xg_ref.at[pl.ds(pads_ref[0, e], n)],
                                         sem.at[2])

        def pad_start(e, carry):
            @pl.when(padn_ref[0, e] > 0)
            def _():
                pad_copy(e).start()
            return carry

        def pad_wait(e, carry):
            @pl.when(padn_ref[0, e] > 0)
            def _():
                pad_copy(e).wait()
            return carry

        n_blocks = xg_ref.shape[0] // R_EXP

        def tail_copy(i):
            return pltpu.make_async_copy(
                zbuf, xg_ref.at[pl.ds(pl.multiple_of(i * R_EXP, R_EXP), R_EXP)], sem.at[2])

        def tail_start(i, carry):
            tail_copy(i).start()
            return carry

        def tail_wait(i, carry):
            tail_copy(i).wait()
            return carry

        lax.fori_loop(0, N_EXPERTS, pad_start, 0)
        lax.fori_loop(nused_ref[0, 0], n_blocks, tail_start, 0)
        lax.fori_loop(0, N_EXPERTS, pad_wait, 0)
        lax.fori_loop(nused_ref[0, 0], n_blocks, tail_wait, 0)
        drain(1 - slot)
        drain(slot)


def _dispatch(dest_blk, cnt_tab, gst_tab, pad_start, pad_len, nused, urows, p_rows):
    t = urows.shape[0]
    n_groups = t // G_TOK
    assert n_groups >= 2
    smem_grp = lambda: pl.BlockSpec((None, 1, N_EXPERTS), lambda i: (i, 0, 0),
                                    memory_space=pltpu.SMEM)
    smem_all = lambda: pl.BlockSpec((1, N_EXPERTS), lambda i: (0, 0), memory_space=pltpu.SMEM)
    return pl.pallas_call(
        _disp_kernel,
        grid=(n_groups,),
        in_specs=[
            pl.BlockSpec((G_TOK * TOP_K,), lambda i: (i,), memory_space=pltpu.SMEM),
            smem_grp(), smem_grp(), smem_all(), smem_all(),
            pl.BlockSpec((1, 1), lambda i: (0, 0), memory_space=pltpu.SMEM),
            pl.BlockSpec((G_TOK, SUBLANES, LANES), lambda i: (i, 0, 0)),
        ],
        out_specs=pl.BlockSpec(memory_space=pl.ANY),
        out_shape=jax.ShapeDtypeStruct((p_rows, SUBLANES, LANES), bf16),
        scratch_shapes=[pltpu.VMEM((2, G_TOK * TOP_K, SUBLANES, LANES), bf16),
                        pltpu.VMEM((R_EXP, SUBLANES, LANES), bf16),
                        pltpu.SemaphoreType.DMA((3,))],
        compiler_params=_cparams(1),
        name="dispatch",
    )(dest_blk, cnt_tab, gst_tab, pad_start, pad_len, nused.reshape(1, 1), urows)


def _experts_kernel(be_ref, nused_ref, x_ref, wgu_ref, wd_ref, y_ref, wgu_b, wd_b):
    rows = x_ref.shape[0] // SUBLANES
    d = wgu_ref.shape[0]
    i = pl.program_id(0)

    @pl.when((i == 0) | (be_ref[i] != be_ref[jnp.maximum(i - 1, 0)]))
    def _():
        wgu_b[...] = wgu_ref[...].astype(bf16)
        wd_b[...] = wd_ref[...].astype(bf16)

    @pl.when(i < nused_ref[0])
    def _():
        xw = x_ref.bitcast(jnp.int32)
        yw = y_ref.bitcast(jnp.int32)
        sub = rows // EXP_SPLIT
        for hf in range(EXP_SPLIT):
            base = hf * sub * PACKED_SUBLANES
            chunks = []
            for jp in range(d // LANES // 2):
                chunks.extend(_unpack_bf16_pair(xw[pl.ds(base + jp, sub, stride=PACKED_SUBLANES), :]))
            x = jnp.concatenate(chunks, axis=1)
            gu = _dot(x, wgu_b[...])
            de = wd_b.shape[0]
            act = (jax.nn.silu(gu[:, :de]) * gu[:, de:]).astype(bf16)
            y = _dot(act, wd_b[...]).astype(bf16)
            for jp in range(d // LANES // 2):
                yw[pl.ds(base + jp, sub, stride=PACKED_SUBLANES), :] = _pack_bf16_pair(
                    y[:, (2 * jp) * LANES:(2 * jp + 1) * LANES],
                    y[:, (2 * jp + 1) * LANES:(2 * jp + 2) * LANES])

    @pl.when(i >= nused_ref[0])
    def _():
        y_ref[...] = jnp.zeros_like(y_ref)


def _experts(block_exp, nused, xg2, w_gu, w_down, layer):
    p8 = xg2.shape[0]
    _, n_e, d, de2 = w_gu.shape
    rb = R_EXP * SUBLANES
    nb = p8 // rb
    grid_spec = pltpu.PrefetchScalarGridSpec(
        num_scalar_prefetch=2,
        grid=(nb,),
        in_specs=[
            pl.BlockSpec((rb, LANES), lambda i, be, nu: (jnp.minimum(i, nu[0] - 1), 0)),
            pl.BlockSpec((None, None, d, de2), lambda i, be, nu: (layer, be[i], 0, 0)),
            pl.BlockSpec((None, None, de2 // 2, d), lambda i, be, nu: (layer, be[i], 0, 0)),
        ],
        out_specs=pl.BlockSpec((rb, LANES), lambda i, be, nu: (i, 0)),
        scratch_shapes=[pltpu.VMEM((d, de2), bf16), pltpu.VMEM((de2 // 2, d), bf16)],
    )
    return pl.pallas_call(
        _experts_kernel,
        grid_spec=grid_spec,
        out_shape=jax.ShapeDtypeStruct((p8, LANES), bf16),
        compiler_params=_cparams(1),
        name="experts",
    )(block_exp, nused, xg2, w_gu, w_down)


def _comb_kernel(dest_ref, wts_ref, cnt_ref, gst_ref, cntn_ref, gstn_ref, hmid_ref, mod_ref, fg_ref,
                 yg_ref, o_ref, yloc, acc, sem, *, final):
    g_tok, d = hmid_ref.shape
    n_rows = g_tok * TOP_K
    g = pl.program_id(0)
    n_groups = pl.num_programs(0)
    slot = g % 2

    def fetch(cnt_r, gst_r, s):
        def seg(e, ls):
            c = cnt_r[0, e]

            @pl.when(c > 0)
            def _():
                pltpu.make_async_copy(yg_ref.at[pl.ds(gst_r[0, e], c)],
                                      yloc.at[s, pl.ds(ls, c)], sem.at[s]).start()

            return ls + c

        lax.fori_loop(0, N_EXPERTS, seg, 0)

    @pl.when(g == 0)
    def _():
        fetch(cnt_ref, gst_ref, 0)

    @pl.when(g + 1 < n_groups)
    def _():
        fetch(cntn_ref, gstn_ref, 1 - slot)

    pltpu.make_async_copy(yg_ref.at[pl.ds(0, n_rows)], yloc.at[slot], sem.at[slot]).wait()

    yw2 = _packed_rows_view(yloc).at[slot]
    hi_mask = jnp.uint32(0xFFFF0000)

    def reduce(i, carry):
        for uu in range(TOK_UNROLL):
            t = i * TOK_UNROLL + uu
            a_lo = a_hi = None
            for k in range(TOP_K):
                off = pl.multiple_of(dest_ref[t * TOP_K + k] * PACKED_SUBLANES, PACKED_SUBLANES)
                bits = lax.bitcast_convert_type(yw2[pl.ds(off, PACKED_SUBLANES), :], jnp.uint32)
                w = wts_ref[t * TOP_K + k]
                t_lo = lax.bitcast_convert_type(bits << 16, f32) * w
                t_hi = lax.bitcast_convert_type(bits & hi_mask, f32) * w
                a_lo = t_lo if a_lo is None else a_lo + t_lo
                a_hi = t_hi if a_hi is None else a_hi + t_hi
            row0 = pl.multiple_of(t * PACKED_SUBLANES, PACKED_SUBLANES)
            acc[0, pl.ds(row0, PACKED_SUBLANES), :] = a_lo
            acc[1, pl.ds(row0, PACKED_SUBLANES), :] = a_hi
        return carry

    lax.fori_loop(0, g_tok // TOK_UNROLL, reduce, 0)
    for j in range(d // LANES):
        cs = slice(j * LANES, (j + 1) * LANES)
        o_ref[:, cs] = hmid_ref[:, cs] + mod_ref[:, 5 * d + j * LANES:5 * d + (j + 1) * LANES] * \
            acc[j % 2, pl.ds(j // 2, g_tok, stride=PACKED_SUBLANES), :]
    if final:
        o_ref[...] = _rmsnorm_rows(o_ref[...], fg_ref[...])


def _combine(dest_blk, wts_blk, cnt_tab, gst_tab, hmid, mod, final_g, yg, final):
    t, d = hmid.shape
    n_groups = t // G_TOK
    gpb = n_groups // mod.shape[0]
    row = pl.BlockSpec((G_TOK, d), lambda i: (i, 0))
    smem_tok = lambda: pl.BlockSpec((G_TOK * TOP_K,), lambda i: (i,), memory_space=pltpu.SMEM)
    smem_cur = lambda: pl.BlockSpec((None, 1, N_EXPERTS), lambda i: (i, 0, 0),
                                    memory_space=pltpu.SMEM)
    smem_nxt = lambda: pl.BlockSpec((None, 1, N_EXPERTS),
                                    lambda i: (jnp.minimum(i + 1, n_groups - 1), 0, 0),
                                    memory_space=pltpu.SMEM)
    return pl.pallas_call(
        functools.partial(_comb_kernel, final=final),
        grid=(n_groups,),
        in_specs=[smem_tok(), smem_tok(), smem_cur(), smem_cur(), smem_nxt(), smem_nxt(), row,
                  pl.BlockSpec((None, 1, mod.shape[-1]), lambda i: (i // gpb, 0, 0)),
                  pl.BlockSpec(final_g.shape, lambda i: (0, 0)),
                  pl.BlockSpec(memory_space=pl.ANY)],
        out_specs=row,
        out_shape=jax.ShapeDtypeStruct((t, d), f32),
        scratch_shapes=[pltpu.VMEM((2, G_TOK * TOP_K, SUBLANES, LANES), bf16),
                        pltpu.VMEM((2, G_TOK * PACKED_SUBLANES, LANES), f32),
                        pltpu.SemaphoreType.DMA((2,))],
        compiler_params=_cparams(1),
        name="combine_final" if final else "combine",
    )(dest_blk, wts_blk, cnt_tab, gst_tab, cnt_tab, gst_tab, hmid, mod, final_g, yg)


def _tok_major(a_t):
    return a_t.T.reshape(-1)


def kernel(x, c, ada_w, ada_b, norm1_g, w_in, conv_w, conv_b, rg_wa, rg_ba, rg_wx, rg_bx, rg_lambda,
           gla_w_up, gla_b_up, gla_norm_g, w_branch_a, w_branch_b, w_out, norm2_g, router_w, router_b,
           exp_w_gu, exp_w_down, shared_w_gu, shared_w_down, final_g):
    bsz, s, d = x.shape
    n_layers = ada_w.shape[0]
    t = bsz * s
    dqk = GLA_HEADS * GLA_DK
    n_assign = t * TOP_K
    nb = -(-(n_assign + N_EXPERTS * (R_EXP - 1)) // R_EXP)
    p = nb * R_EXP
    n_groups = t // G_TOK

    mod_all = _ada(c, ada_w, ada_b)
    h = x
    for l in range(n_layers):
        mod = mod_all[l].reshape(bsz, 1, 6 * d)
        wl = w_in[l]
        lr0 = 2 * d + 2 * dqk + 2 * d
        w_main = jnp.concatenate([wl[:, :lr0], wl[:, lr0 + GLA_RANK:]], axis=1).astype(bf16)
        w_lr = jnp.pad(wl[:, lr0:lr0 + GLA_RANK], ((0, 0), (0, LANES - GLA_RANK))).astype(bf16)
        w_up = jnp.pad(gla_w_up[l], ((0, LANES - GLA_RANK), (0, 0))).astype(bf16)
        xr, gg, q, k, v, og, ga, gb, la = _in_proj(
            h, mod, norm1_g[l].reshape(1, d), w_main, w_lr, w_up, gla_b_up[l].reshape(1, dqk))

        wab = jnp.concatenate([rg_wa[l], rg_wx[l]], axis=-1).astype(bf16)
        ya_in = _rglru(xr, gg, conv_w[l], conv_b[l].reshape(1, d), wab,
                       rg_ba[l].reshape(1, d), rg_bx[l].reshape(1, d), rg_lambda[l].reshape(1, d))
        ob = _gla(q, k, v, la, og, gla_norm_g[l].reshape(1, d))

        wr_hi = router_w[l].astype(bf16)
        wr_lo = (router_w[l] - wr_hi.astype(f32)).astype(bf16)
        hmid, urows, idx_t, wts_t, rank_t, counts = _post(
            ya_in, ob, ga, gb, h, mod,
            w_branch_a[l].astype(bf16), w_branch_b[l].astype(bf16), w_out[l].astype(bf16),
            norm2_g[l].reshape(1, d), shared_w_gu[l].astype(bf16), shared_w_down[l].astype(bf16),
            wr_hi, wr_lo, router_b[l].reshape(N_EXPERTS, 1))

        cnt = counts[:, :, 0].astype(jnp.int32)
        lstart = jnp.cumsum(cnt, axis=1) - cnt
        ctot = jnp.sum(cnt, axis=0)
        padded = (ctot + R_EXP - 1) // R_EXP * R_EXP
        ends = jnp.cumsum(padded)
        pstart = ends - padded
        gstart = pstart[None, :] + jnp.cumsum(cnt, axis=0) - cnt
        blk_row = jnp.arange(nb, dtype=jnp.int32) * R_EXP
        block_exp = jnp.minimum(
            jnp.sum((ends[None, :] <= blk_row[:, None]).astype(jnp.int32), axis=1),
            N_EXPERTS - 1).astype(jnp.int32)
        nused = (ends[-1:] // R_EXP).astype(jnp.int32)
        cnt_tab = cnt.reshape(n_groups, 1, N_EXPERTS)
        gst_tab = gstart.astype(jnp.int32).reshape(n_groups, 1, N_EXPERTS)

        dest_t = _dest(idx_t, rank_t, lstart.astype(f32).reshape(n_groups, N_EXPERTS, 1))
        dest_blk = _tok_major(dest_t)
        xg = _dispatch(dest_blk, cnt_tab, gst_tab,
                       (pstart + ctot).astype(jnp.int32).reshape(1, N_EXPERTS),
                       (padded - ctot).astype(jnp.int32).reshape(1, N_EXPERTS),
                       nused, urows.reshape(t, SUBLANES, LANES), p)
        yg = _experts(block_exp, nused, xg.reshape(p * SUBLANES, LANES), exp_w_gu, exp_w_down, l)
        h = _combine(dest_blk, _tok_major(wts_t), cnt_tab, gst_tab, hmid.reshape(t, d), mod,
                     final_g.reshape(1, d), yg.reshape(p, SUBLANES, LANES),
                     l == n_layers - 1).reshape(bsz, s, d)
    return h
```

```python
import functools

import jax
import jax.numpy as jnp
import numpy as np
from jax import lax
from jax.experimental import pallas as pl
from jax.experimental.pallas import tpu as pltpu

EPS = 1e-6
CHUNK = 64
CONV_W = 4
RG_C = 8.0
RNN_BLOCKS = 8
GLA_HEADS = 4
GLA_DK = 128
GLA_DV = 256
GLA_RANK = 16
GLA_TAU = 16.0
N_EXPERTS = 128
TOP_K = 8
D_EXPERT = 256
ROUTED_SCALE = 2.5

LANES = 128
SUBLANES = 8
PACKED_SUBLANES = 4
VMEM_LIMIT = 56 * 1024 * 1024

TM_IN = 512
TS_RNN = 512
TS_GLA = 512
G_TOK = 512
POST_SPLIT = 2
R_EXP = 1024
EXP_SPLIT = 2

bf16 = jnp.bfloat16
f32 = jnp.float32


def _cparams(n_axes):
    return pltpu.CompilerParams(
        dimension_semantics=("arbitrary",) * n_axes,
        vmem_limit_bytes=VMEM_LIMIT,
    )


def _dot(a, b):
    return jnp.dot(a, b, preferred_element_type=f32)


def _dot_nt(a, b):
    return lax.dot_general(a, b, (((1,), (1,)), ((), ())), preferred_element_type=f32)


def _dot_tn(a, b):
    return lax.dot_general(a, b, (((0,), (0,)), ((), ())), preferred_element_type=f32)


def _split_bf16(x):
    hi = x.astype(bf16)
    lo = (x - hi.astype(f32)).astype(bf16)
    return hi, lo


def _pack_bf16_pair(lo, hi):
    lo_bits = lax.bitcast_convert_type(lo.astype(f32), jnp.uint32) >> 16
    hi_bits = lax.bitcast_convert_type(hi.astype(f32), jnp.uint32) & jnp.uint32(0xFFFF0000)
    return lax.bitcast_convert_type(hi_bits | lo_bits, jnp.int32)


def _unpack_bf16_pair(word):
    bits = lax.bitcast_convert_type(word, jnp.uint32)
    lo = lax.bitcast_convert_type(bits << 16, f32).astype(bf16)
    hi = lax.bitcast_convert_type(bits & jnp.uint32(0xFFFF0000), f32).astype(bf16)
    return lo, hi


def _packed_rows_view(buf):
    two, n, sl, ln = buf.shape
    return (buf.reshape(two, n // 2, 2 * sl, ln).bitcast(jnp.int32)
            .reshape(two, n * PACKED_SUBLANES, ln))


def _rmsnorm_rows(x, g):
    return x * lax.rsqrt(jnp.mean(x * x, axis=-1, keepdims=True) + EPS) * g


def _log_sigmoid(x):
    return jnp.minimum(x, 0.0) - jnp.log1p(jnp.exp(-jnp.abs(x)))


def _softplus(x):
    return jnp.maximum(x, 0.0) + jnp.log1p(jnp.exp(-jnp.abs(x)))


def _ada_kernel(c_ref, w_ref, b_ref, o_ref):
    c_act = jax.nn.silu(c_ref[...])
    c_hi, c_lo = _split_bf16(c_act)
    w_hi, w_lo = _split_bf16(w_ref[...])
    acc = _dot(c_hi, w_hi) + _dot(c_lo, w_hi) + _dot(c_hi, w_lo)
    o_ref[...] = acc + b_ref[...]


def _ada(c, ada_w, ada_b):
    n_layers, d, n6 = ada_w.shape
    bsz = c.shape[0]
    tn = 512
    return pl.pallas_call(
        _ada_kernel,
        grid=(n_layers, n6 // tn),
        in_specs=[
            pl.BlockSpec((bsz, d), lambda l, j: (0, 0)),
            pl.BlockSpec((None, d, tn), lambda l, j: (l, 0, j)),
            pl.BlockSpec((None, 1, tn), lambda l, j: (l, 0, j)),
        ],
        out_specs=pl.BlockSpec((None, bsz, tn), lambda l, j: (l, 0, j)),
        out_shape=jax.ShapeDtypeStruct((n_layers, bsz, n6), f32),
        compiler_params=_cparams(2),
        name="ada_mod",
    )(c, ada_w, ada_b.reshape(n_layers, 1, n6))


def _in_kernel(h_ref, mod_ref, g_ref, w_ref, wlr_ref, wup_ref, bup_ref,
               xr_ref, gg_ref, q_ref, k_ref, v_ref, og_ref, ga_ref, gb_ref, la_ref, u_scr):
    d = h_ref.shape[-1]
    sh1 = mod_ref[:, 0:d]
    sc1 = mod_ref[:, d:2 * d]
    u = _rmsnorm_rows(h_ref[...], g_ref[...]) * (1.0 + sc1) + sh1
    u_scr[...] = u.astype(bf16)

    def proj(c0, width):
        return _dot(u_scr[...], w_ref[:, c0:c0 + width])

    cw = 512
    for c in range(d // cw):
        xr_ref[:, c * cw:(c + 1) * cw] = proj(c * cw, cw)
    for c in range(d // cw):
        y = proj(d + c * cw, cw)
        gg_ref[:, c * cw:(c + 1) * cw] = jax.nn.gelu(y, approximate=True).astype(bf16)
    dqk = GLA_HEADS * GLA_DK
    q_ref[...] = (proj(2 * d, dqk) * (GLA_DK ** -0.5)).astype(bf16)
    k_ref[...] = proj(2 * d + dqk, dqk).astype(bf16)
    base = 2 * d + 2 * dqk
    for c in range(d // cw):
        v_ref[:, c * cw:(c + 1) * cw] = proj(base + c * cw, cw).astype(bf16)
    base += d
    for c in range(d // cw):
        y = proj(base + c * cw, cw)
        og_ref[:, c * cw:(c + 1) * cw] = jax.nn.silu(y).astype(bf16)
    base += d
    for c in range(d // cw):
        y = proj(base + c * cw, cw)
        ga_ref[:, c * cw:(c + 1) * cw] = jax.nn.sigmoid(y).astype(bf16)
    base += d
    for c in range(d // cw):
        y = proj(base + c * cw, cw)
        gb_ref[:, c * cw:(c + 1) * cw] = jax.nn.sigmoid(y).astype(bf16)
    lr = _dot(u_scr[...], wlr_ref[...]).astype(bf16)
    logit = _dot(lr, wup_ref[...]) + bup_ref[...]
    la_ref[...] = _log_sigmoid(logit) * (1.0 / GLA_TAU)


def _in_proj(h, mod, norm_g, w_main, w_lr, w_up, b_up):
    bsz, s, d = h.shape
    tm = TM_IN
    dqk = GLA_HEADS * GLA_DK
    row = lambda width: pl.BlockSpec((None, tm, width), lambda b, i: (b, i, 0))
    full = lambda a: pl.BlockSpec(a.shape, lambda b, i: (0,) * a.ndim)
    out_shapes = [
        jax.ShapeDtypeStruct((bsz, s, d), f32),
        jax.ShapeDtypeStruct((bsz, s, d), bf16),
        jax.ShapeDtypeStruct((bsz, s, dqk), bf16),
        jax.ShapeDtypeStruct((bsz, s, dqk), bf16),
        jax.ShapeDtypeStruct((bsz, s, d), bf16),
        jax.ShapeDtypeStruct((bsz, s, d), bf16),
        jax.ShapeDtypeStruct((bsz, s, d), bf16),
        jax.ShapeDtypeStruct((bsz, s, d), bf16),
        jax.ShapeDtypeStruct((bsz, s, dqk), f32),
    ]
    return pl.pallas_call(
        _in_kernel,
        grid=(bsz, s // tm),
        in_specs=[
            row(d),
            pl.BlockSpec((None, 1, mod.shape[-1]), lambda b, i: (b, 0, 0)),
            full(norm_g),
            pl.BlockSpec(w_main.shape, lambda b, i: (0, 0), pipeline_mode=pl.Buffered(1)),
            full(w_lr), full(w_up), full(b_up),
        ],
        out_specs=[row(d), row(d), row(dqk), row(dqk), row(d), row(d), row(d), row(d), row(dqk)],
        out_shape=out_shapes,
        scratch_shapes=[pltpu.VMEM((tm, d), bf16)],
        compiler_params=_cparams(2),
        name="in_proj",
    )(h, mod, norm_g, w_main, w_lr, w_up, b_up)


def _rglru_kernel(xr_ref, gg_ref, cw_ref, cb_ref, wab_ref, ba_ref, bx_ref, lam_ref,
                  o_ref, xpad, hcar):
    ts, d = xr_ref.shape
    si = pl.program_id(1)

    @pl.when(si == 0)
    def _():
        xpad[0:SUBLANES, :] = jnp.zeros((SUBLANES, d), f32)
        hcar[...] = jnp.zeros_like(hcar)

    xpad[SUBLANES:SUBLANES + ts, :] = xr_ref[...]
    xc = cb_ref[...] + xpad[SUBLANES:SUBLANES + ts, :] * cw_ref[CONV_W - 1:CONV_W, :]
    for j in range(CONV_W - 1):
        off = SUBLANES - (CONV_W - 1) + j
        xc = xc + xpad[off:off + ts, :] * cw_ref[j:j + 1, :]
    xpad[0:SUBLANES, :] = xpad[ts:ts + SUBLANES, :]

    groups = ts // SUBLANES
    bw = d // RNN_BLOCKS
    sub = lax.broadcasted_iota(jnp.int32, (groups, SUBLANES, bw), 1)
    first_row = (lax.broadcasted_iota(jnp.int32, (ts, bw), 0) == 0) & (si == 0)
    sp = _softplus(-lam_ref[...])
    xcb = xc.astype(bf16)
    for hb in range(RNN_BLOCKS):
        cs = slice(hb * bw, (hb + 1) * bw)
        x_h = xc[:, cs]
        g = _dot(xcb[:, cs], wab_ref[hb])
        r = jax.nn.sigmoid(g[:, :bw] + ba_ref[:, cs])
        i = jax.nn.sigmoid(g[:, bw:] + bx_ref[:, cs])
        log_a = (-RG_C) * r * sp[:, cs]
        a = jnp.exp(log_a)
        mult = jnp.sqrt(1.0 - a * a)
        mult = jnp.where(first_row, 1.0, mult)
        bt = mult * (i * x_h)
        av = a.reshape(groups, SUBLANES, bw)
        bv = bt.reshape(groups, SUBLANES, bw)
        for sh in (1, 2, 4):
            a_prev = pltpu.roll(av, sh, 1)
            b_prev = pltpu.roll(bv, sh, 1)
            keep = sub >= sh
            bv = jnp.where(keep, av * b_prev + bv, bv)
            av = jnp.where(keep, av * a_prev, av)
        hin = hcar[0:1, cs]
        outs = []
        for gi in range(groups):
            hg = av[gi] * hin + bv[gi]
            outs.append(hg)
            hin = hg[SUBLANES - 1:SUBLANES, :]
        hcar[0:1, cs] = hin
        hs = jnp.concatenate(outs, axis=0)
        o_ref[:, cs] = (hs * gg_ref[:, cs].astype(f32)).astype(bf16)


def _rglru(xr, gg, conv_w, conv_b, wab, ba, bx, lam):
    bsz, s, d = xr.shape
    ts = TS_RNN
    row = pl.BlockSpec((None, ts, d), lambda b, i: (b, i, 0))
    full = lambda a: pl.BlockSpec(a.shape, lambda b, i: (0,) * a.ndim)
    return pl.pallas_call(
        _rglru_kernel,
        grid=(bsz, s // ts),
        in_specs=[row, row, full(conv_w), full(conv_b), full(wab), full(ba), full(bx), full(lam)],
        out_specs=row,
        out_shape=jax.ShapeDtypeStruct((bsz, s, d), bf16),
        scratch_shapes=[pltpu.VMEM((ts + SUBLANES, d), f32), pltpu.VMEM((SUBLANES, d), f32)],
        compiler_params=_cparams(2),
        name="rglru",
    )(xr, gg, conv_w, conv_b, wab, ba, bx, lam)


def _gla_kernel(q_ref, k_ref, v_ref, la_ref, og_ref, g_ref, o_ref, st):
    ts = q_ref.shape[0]
    si = pl.program_id(1)

    @pl.when(si == 0)
    def _():
        st[...] = jnp.zeros_like(st)

    ri = lax.broadcasted_iota(jnp.int32, (CHUNK, CHUNK), 0)
    ci = lax.broadcasted_iota(jnp.int32, (CHUNK, CHUNK), 1)
    tri = jnp.where(ci <= ri, 1.0, 0.0).astype(bf16)
    for c in range(ts // CHUNK):
        rs = slice(c * CHUNK, (c + 1) * CHUNK)
        la_hi, la_lo = _split_bf16(la_ref[rs, :])
        cum = _dot(tri, la_hi) + _dot(tri, la_lo)
        tot = cum[CHUNK - 1:CHUNK, :]
        kdec = (k_ref[rs, :].astype(f32) * jnp.exp(tot - cum)).astype(bf16)
        dec = jnp.exp(tot)
        for hd in range(GLA_HEADS):
            ks = slice(hd * GLA_DK, (hd + 1) * GLA_DK)
            vs = slice(hd * GLA_DV, (hd + 1) * GLA_DV)
            s_new = st[hd] * dec[:, ks] + _dot_tn(v_ref[rs, vs], kdec[:, ks])
            st[hd] = s_new
            o = _dot_nt(q_ref[rs, ks], s_new.astype(bf16))
            o = _rmsnorm_rows(o, g_ref[:, vs])
            o_ref[rs, vs] = (o * og_ref[rs, vs].astype(f32)).astype(bf16)


def _gla(q, k, v, la, og, norm_g):
    bsz, s, dqk = q.shape
    d = v.shape[-1]
    ts = TS_GLA
    row = lambda width: pl.BlockSpec((None, ts, width), lambda b, i: (b, i, 0))
    return pl.pallas_call(
        _gla_kernel,
        grid=(bsz, s // ts),
        in_specs=[row(dqk), row(dqk), row(d), row(dqk), row(d),
                  pl.BlockSpec(norm_g.shape, lambda b, i: (0, 0))],
        out_specs=row(d),
        out_shape=jax.ShapeDtypeStruct((bsz, s, d), bf16),
        scratch_shapes=[pltpu.VMEM((GLA_HEADS, GLA_DV, GLA_DK), f32)],
        compiler_params=_cparams(2),
        name="gla",
    )(q, k, v, la, og, norm_g)


def _post_kernel(ya_ref, ob_ref, ga_ref, gb_ref, h_ref, mod_ref, wa_ref, wb_ref, wo_ref, g2_ref,
                 wsgu_ref, wsd_ref, wrh_ref, wrl_ref, rb_ref,
                 hmid_ref, urow_ref, idx_ref, wts_ref, rank_ref, cnt_ref):
    tm, d = h_ref.shape
    sub = tm // POST_SPLIT
    g1 = mod_ref[:, 2 * d:3 * d]
    sh2 = mod_ref[:, 3 * d:4 * d]
    sc2 = mod_ref[:, 4 * d:5 * d]
    g2 = mod_ref[:, 5 * d:6 * d]
    n_e = wrh_ref.shape[1]
    e_iota = lax.broadcasted_iota(jnp.int32, (n_e, sub), 0).astype(f32)
    ri = lax.broadcasted_iota(jnp.int32, (sub, sub), 0)
    ci = lax.broadcasted_iota(jnp.int32, (sub, sub), 1)
    upper = jnp.where(ri <= ci, 1.0, 0.0).astype(bf16)
    running = jnp.zeros((n_e, LANES), f32)
    for hf in range(POST_SPLIT):
        rs = slice(hf * sub, (hf + 1) * sub)
        ya = _dot(ya_ref[rs, :], wa_ref[...])
        yb = _dot(ob_ref[rs, :], wb_ref[...])
        m = (ga_ref[rs, :].astype(f32) * ya + gb_ref[rs, :].astype(f32) * yb).astype(bf16)
        h1 = h_ref[rs, :] + g1 * _dot(m, wo_ref[...])
        u2 = _rmsnorm_rows(h1, g2_ref[...]) * (1.0 + sc2) + sh2
        u_hi, u_lo = _split_bf16(u2)

        gu = _dot(u_hi, wsgu_ref[...])
        dsh = wsd_ref.shape[0]
        act = (jax.nn.silu(gu[:, :dsh]) * gu[:, dsh:]).astype(bf16)
        hmid_ref[rs, :] = h1 + g2 * _dot(act, wsd_ref[...])

        uw = urow_ref.bitcast(jnp.int32)
        for jp in range(d // LANES // 2):
            word = _pack_bf16_pair(u_hi[:, (2 * jp) * LANES:(2 * jp + 1) * LANES],
                                   u_hi[:, (2 * jp + 1) * LANES:(2 * jp + 2) * LANES])
            uw[pl.ds(hf * sub * PACKED_SUBLANES + jp, sub, stride=PACKED_SUBLANES), :] = word

        logits = (_dot(u_hi, wrh_ref[...]) + _dot(u_lo, wrh_ref[...])
                  + _dot(u_hi, wrl_ref[...])).T
        scores = jax.nn.sigmoid(logits)
        work = scores + rb_ref[...]
        onehots, idxs, svals = [], [], []
        for _ in range(TOP_K):
            mx = jnp.max(work, axis=0, keepdims=True)
            ik = jnp.min(jnp.where(work == mx, e_iota, float(n_e)), axis=0, keepdims=True)
            oh = e_iota == ik
            onehots.append(oh)
            idxs.append(ik.astype(jnp.int32))
            svals.append(jnp.sum(jnp.where(oh, scores, 0.0), axis=0, keepdims=True))
            work = jnp.where(oh, -jnp.inf, work)
        ssum = svals[0]
        for sv in svals[1:]:
            ssum = ssum + sv
        idx_ref[:, rs] = jnp.concatenate(idxs, axis=0)
        wts_ref[:, rs] = jnp.concatenate([sv / ssum * ROUTED_SCALE for sv in svals], axis=0)

        member = jnp.where(onehots[0], 1.0, 0.0)
        for oh in onehots[1:]:
            member = member + jnp.where(oh, 1.0, 0.0)
        incl = _dot(member.astype(bf16), upper)
        base = jnp.concatenate([running] * (sub // LANES), axis=1)
        pos = base + incl - member
        ranks = [jnp.sum(jnp.where(oh, pos, 0.0), axis=0, keepdims=True) for oh in onehots]
        rank_ref[:, rs] = jnp.concatenate(ranks, axis=0).astype(jnp.int32)
        running = running + jnp.broadcast_to(incl[:, sub - 1:sub], running.shape)
    cnt_ref[...] = running


def _post(ya_in, ob, ga, gb, h, mod, wa, wb, wo, g2, wsgu, wsd, wr_hi, wr_lo, rb):
    bsz, s, d = h.shape
    tm = G_TOK
    t = bsz * s
    nsb = s // tm
    n_groups = t // G_TOK
    row = pl.BlockSpec((None, tm, d), lambda b, i: (b, i, 0))
    full = lambda a: pl.BlockSpec(a.shape, lambda b, i: (0,) * a.ndim)
    tok = pl.BlockSpec((TOP_K, tm), lambda b, i: (0, b * nsb + i))
    return pl.pallas_call(
        _post_kernel,
        grid=(bsz, nsb),
        in_specs=[row, row, row, row, row,
                  pl.BlockSpec((None, 1, mod.shape[-1]), lambda b, i: (b, 0, 0)),
                  full(wa), full(wb), full(wo), full(g2), full(wsgu), full(wsd),
                  full(wr_hi), full(wr_lo), full(rb)],
        out_specs=[row,
                   pl.BlockSpec((tm * SUBLANES, LANES), lambda b, i: (b * nsb + i, 0)),
                   tok, tok, tok,
                   pl.BlockSpec((None, N_EXPERTS, LANES), lambda b, i: (b * nsb + i, 0, 0))],
        out_shape=[jax.ShapeDtypeStruct((bsz, s, d), f32),
                   jax.ShapeDtypeStruct((t * SUBLANES, LANES), bf16),
                   jax.ShapeDtypeStruct((TOP_K, t), jnp.int32),
                   jax.ShapeDtypeStruct((TOP_K, t), f32),
                   jax.ShapeDtypeStruct((TOP_K, t), jnp.int32),
                   jax.ShapeDtypeStruct((n_groups, N_EXPERTS, LANES), f32)],
        compiler_params=_cparams(2),
        name="post",
    )(ya_in, ob, ga, gb, h, mod, wa, wb, wo, g2, wsgu, wsd, wr_hi, wr_lo, rb)


def _dest_kernel(idx_ref, rank_ref, pstart_ref, o_ref):
    tn = idx_ref.shape[1]
    n_e = pstart_ref.shape[0]
    e_iota = lax.broadcasted_iota(jnp.int32, (n_e, tn), 0)
    rows = []
    for k in range(TOP_K):
        start = jnp.sum(jnp.where(e_iota == idx_ref[k:k + 1, :], pstart_ref[...], 0.0),
                        axis=0, keepdims=True)
        rows.append(start.astype(jnp.int32) + rank_ref[k:k + 1, :])
    o_ref[...] = jnp.concatenate(rows, axis=0)


def _dest(idx_t, rank_t, lstart_col):
    t = idx_t.shape[1]
    tn = G_TOK
    tok = pl.BlockSpec((TOP_K, tn), lambda i: (0, i))
    return pl.pallas_call(
        _dest_kernel,
        grid=(t // tn,),
        in_specs=[tok, tok, pl.BlockSpec((None, N_EXPERTS, 1), lambda i: (i, 0, 0))],
        out_specs=tok,
        out_shape=jax.ShapeDtypeStruct((TOP_K, t), jnp.int32),
        compiler_params=_cparams(1),
        name="dest",
    )(idx_t, rank_t, lstart_col)


TOK_UNROLL = 4


def _table_entry(tab_ref, i, c):
    return tab_ref[i * (TOK_UNROLL * TOP_K) + c]


def _disp_kernel(dest_ref, cnt_ref, gst_ref, pads_ref, padn_ref, nused_ref, u_ref, xg_ref,
                 xloc, zbuf, sem):
    g_tok = u_ref.shape[0]
    n_rows = g_tok * TOP_K
    g = pl.program_id(0)
    last = pl.num_programs(0) - 1
    slot = g % 2

    def drain(s):
        pltpu.make_async_copy(xloc.at[s], xg_ref.at[pl.ds(0, n_rows)], sem.at[s]).wait()

    @pl.when(g == 0)
    def _():
        zbuf[...] = jnp.zeros_like(zbuf)

    @pl.when(g >= 2)
    def _():
        drain(slot)

    xs = xloc.at[slot]
    uw = u_ref.bitcast(jnp.int32)
    xw2 = _packed_rows_view(xloc).at[slot]

    def fill(i, carry):
        for uu in range(TOK_UNROLL):
            t = i * TOK_UNROLL + uu
            row = uw[t]
            for k in range(TOP_K):
                slot_row = _table_entry(dest_ref, i, uu * TOP_K + k)
                off = pl.multiple_of(slot_row * PACKED_SUBLANES, PACKED_SUBLANES)
                xw2[pl.ds(off, PACKED_SUBLANES), :] = row
        return carry

    lax.fori_loop(0, g_tok // TOK_UNROLL, fill, 0)

    def seg(e, ls):
        c = cnt_ref[0, e]

        @pl.when(c > 0)
        def _():
            pltpu.make_async_copy(xs.at[pl.ds(ls, c)], xg_ref.at[pl.ds(gst_ref[0, e], c)],
                                  sem.at[slot]).start()

        return ls + c

    lax.fori_loop(0, N_EXPERTS, seg, 0)

    @pl.when(g == last)
    def _():
        def pad_copy(e):
            n = padn_ref[0, e]
            return pltpu.make_async_copy(zbuf.at[pl.ds(0, n)], xg_ref.at[pl.ds(pads_ref[0, e], n)],
                                         sem.at[2])

        def pad_start(e, carry):
            @pl.when(padn_ref[0, e] > 0)
            def _():
                pad_copy(e).start()
            return carry

        def pad_wait(e, carry):
            @pl.when(padn_ref[0, e] > 0)
            def _():
                pad_copy(e).wait()
            return carry

        n_blocks = xg_ref.shape[0] // R_EXP

        def tail_copy(i):
            return pltpu.make_async_copy(
                zbuf, xg_ref.at[pl.ds(pl.multiple_of(i * R_EXP, R_EXP), R_EXP)], sem.at[2])

        def tail_start(i, carry):
            tail_copy(i).start()
            return carry

        def tail_wait(i, carry):
            tail_copy(i).wait()
            return carry

        lax.fori_loop(0, N_EXPERTS, pad_start, 0)
        lax.fori_loop(nused_ref[0, 0], n_blocks, tail_start, 0)
        lax.fori_loop(0, N_EXPERTS, pad_wait, 0)
        lax.fori_loop(nused_ref[0, 0], n_blocks, tail_wait, 0)
        drain(1 - slot)
        drain(slot)


def _dispatch(dest_blk, cnt_tab, gst_tab, pad_start, pad_len, nused, urows, p_rows):
    t = urows.shape[0]
    n_groups = t // G_TOK
    assert n_groups >= 2
    smem_grp = lambda: pl.BlockSpec((None, 1, N_EXPERTS), lambda i: (i, 0, 0),
                                    memory_space=pltpu.SMEM)
    smem_all = lambda: pl.BlockSpec((1, N_EXPERTS), lambda i: (0, 0), memory_space=pltpu.SMEM)
    return pl.pallas_call(
        _disp_kernel,
        grid=(n_groups,),
        in_specs=[
            pl.BlockSpec((G_TOK * TOP_K,), lambda i: (i,), memory_space=pltpu.SMEM),
            smem_grp(), smem_grp(), smem_all(), smem_all(),
            pl.BlockSpec((1, 1), lambda i: (0, 0), memory_space=pltpu.SMEM),
            pl.BlockSpec((G_TOK, SUBLANES, LANES), lambda i: (i, 0, 0)),
        ],
        out_specs=pl.BlockSpec(memory_space=pl.ANY),
        out_shape=jax.ShapeDtypeStruct((p_rows, SUBLANES, LANES), bf16),
        scratch_shapes=[pltpu.VMEM((2, G_TOK * TOP_K, SUBLANES, LANES), bf16),
                        pltpu.VMEM((R_EXP, SUBLANES, LANES), bf16),
                        pltpu.SemaphoreType.DMA((3,))],
        compiler_params=_cparams(1),
        name="dispatch",
    )(dest_blk, cnt_tab, gst_tab, pad_start, pad_len, nused.reshape(1, 1), urows)


def _experts_kernel(be_ref, nused_ref, x_ref, wgu_ref, wd_ref, y_ref, wgu_b, wd_b):
    rows = x_ref.shape[0] // SUBLANES
    d = wgu_ref.shape[0]
    i = pl.program_id(0)

    @pl.when((i == 0) | (be_ref[i] != be_ref[jnp.maximum(i - 1, 0)]))
    def _():
        wgu_b[...] = wgu_ref[...].astype(bf16)
        wd_b[...] = wd_ref[...].astype(bf16)

    @pl.when(i < nused_ref[0])
    def _():
        xw = x_ref.bitcast(jnp.int32)
        yw = y_ref.bitcast(jnp.int32)
        sub = rows // EXP_SPLIT
        for hf in range(EXP_SPLIT):
            base = hf * sub * PACKED_SUBLANES
            gu = None
            for jp in range(d // LANES // 2):
                lo, hi = _unpack_bf16_pair(xw[pl.ds(base + jp, sub, stride=PACKED_SUBLANES), :])
                part = _dot(jnp.concatenate([lo, hi], axis=1),
                            wgu_b[2 * jp * LANES:(2 * jp + 2) * LANES, :])
                gu = part if gu is None else gu + part
            de = wd_b.shape[0]
            act = (jax.nn.silu(gu[:, :de]) * gu[:, de:]).astype(bf16)
            y = _dot(act, wd_b[...]).astype(bf16)
            for jp in range(d // LANES // 2):
                yw[pl.ds(base + jp, sub, stride=PACKED_SUBLANES), :] = _pack_bf16_pair(
                    y[:, (2 * jp) * LANES:(2 * jp + 1) * LANES],
                    y[:, (2 * jp + 1) * LANES:(2 * jp + 2) * LANES])

    @pl.when(i >= nused_ref[0])
    def _():
        y_ref[...] = jnp.zeros_like(y_ref)


def _experts(block_exp, nused, xg2, w_gu, w_down, layer):
    p8 = xg2.shape[0]
    _, n_e, d, de2 = w_gu.shape
    rb = R_EXP * SUBLANES
    nb = p8 // rb
    grid_spec = pltpu.PrefetchScalarGridSpec(
        num_scalar_prefetch=2,
        grid=(nb,),
        in_specs=[
            pl.BlockSpec((rb, LANES), lambda i, be, nu: (jnp.minimum(i, nu[0] - 1), 0)),
            pl.BlockSpec((None, None, d, de2), lambda i, be, nu: (layer, be[i], 0, 0)),
            pl.BlockSpec((None, None, de2 // 2, d), lambda i, be, nu: (layer, be[i], 0, 0)),
        ],
        out_specs=pl.BlockSpec((rb, LANES), lambda i, be, nu: (i, 0)),
        scratch_shapes=[pltpu.VMEM((d, de2), bf16), pltpu.VMEM((de2 // 2, d), bf16)],
    )
    return pl.pallas_call(
        _experts_kernel,
        grid_spec=grid_spec,
        out_shape=jax.ShapeDtypeStruct((p8, LANES), bf16),
        compiler_params=_cparams(1),
        name="experts",
    )(block_exp, nused, xg2, w_gu, w_down)


def _comb_kernel(dest_ref, wts_ref, cnt_ref, gst_ref, cntn_ref, gstn_ref, hmid_ref, mod_ref, fg_ref,
                 yg_ref, o_ref, yloc, acc, sem, *, final):
    g_tok, d = hmid_ref.shape
    n_rows = g_tok * TOP_K
    g = pl.program_id(0)
    n_groups = pl.num_programs(0)
    slot = g % 2

    def fetch(cnt_r, gst_r, s):
        def seg(e, ls):
            c = cnt_r[0, e]

            @pl.when(c > 0)
            def _():
                pltpu.make_async_copy(yg_ref.at[pl.ds(gst_r[0, e], c)],
                                      yloc.at[s, pl.ds(ls, c)], sem.at[s]).start()

            return ls + c

        lax.fori_loop(0, N_EXPERTS, seg, 0)

    @pl.when(g == 0)
    def _():
        fetch(cnt_ref, gst_ref, 0)

    @pl.when(g + 1 < n_groups)
    def _():
        fetch(cntn_ref, gstn_ref, 1 - slot)

    pltpu.make_async_copy(yg_ref.at[pl.ds(0, n_rows)], yloc.at[slot], sem.at[slot]).wait()

    yw2 = _packed_rows_view(yloc).at[slot]
    hi_mask = jnp.uint32(0xFFFF0000)

    def reduce(i, carry):
        for uu in range(TOK_UNROLL):
            t = i * TOK_UNROLL + uu
            a_lo = a_hi = None
            for k in range(TOP_K):
                slot_row = _table_entry(dest_ref, i, uu * TOP_K + k)
                off = pl.multiple_of(slot_row * PACKED_SUBLANES, PACKED_SUBLANES)
                bits = lax.bitcast_convert_type(yw2[pl.ds(off, PACKED_SUBLANES), :], jnp.uint32)
                w = _table_entry(wts_ref, i, uu * TOP_K + k)
                t_lo = lax.bitcast_convert_type(bits << 16, f32) * w
                t_hi = lax.bitcast_convert_type(bits & hi_mask, f32) * w
                a_lo = t_lo if a_lo is None else a_lo + t_lo
                a_hi = t_hi if a_hi is None else a_hi + t_hi
            row0 = pl.multiple_of(t * PACKED_SUBLANES, PACKED_SUBLANES)
            acc[0, pl.ds(row0, PACKED_SUBLANES), :] = a_lo
            acc[1, pl.ds(row0, PACKED_SUBLANES), :] = a_hi
        return carry

    lax.fori_loop(0, g_tok // TOK_UNROLL, reduce, 0)
    for j in range(d // LANES):
        cs = slice(j * LANES, (j + 1) * LANES)
        o_ref[:, cs] = hmid_ref[:, cs] + mod_ref[:, 5 * d + j * LANES:5 * d + (j + 1) * LANES] * \
            acc[j % 2, pl.ds(j // 2, g_tok, stride=PACKED_SUBLANES), :]
    if final:
        o_ref[...] = _rmsnorm_rows(o_ref[...], fg_ref[...])


def _combine(dest_blk, wts_blk, cnt_tab, gst_tab, hmid, mod, final_g, yg, final):
    t, d = hmid.shape
    n_groups = t // G_TOK
    gpb = n_groups // mod.shape[0]
    row = pl.BlockSpec((G_TOK, d), lambda i: (i, 0))
    smem_tok = lambda: pl.BlockSpec((G_TOK * TOP_K,), lambda i: (i,), memory_space=pltpu.SMEM)
    smem_cur = lambda: pl.BlockSpec((None, 1, N_EXPERTS), lambda i: (i, 0, 0),
                                    memory_space=pltpu.SMEM)
    smem_nxt = lambda: pl.BlockSpec((None, 1, N_EXPERTS),
                                    lambda i: (jnp.minimum(i + 1, n_groups - 1), 0, 0),
                                    memory_space=pltpu.SMEM)
    return pl.pallas_call(
        functools.partial(_comb_kernel, final=final),
        grid=(n_groups,),
        in_specs=[smem_tok(), smem_tok(), smem_cur(), smem_cur(), smem_nxt(), smem_nxt(), row,
                  pl.BlockSpec((None, 1, mod.shape[-1]), lambda i: (i // gpb, 0, 0)),
                  pl.BlockSpec(final_g.shape, lambda i: (0, 0)),
                  pl.BlockSpec(memory_space=pl.ANY)],
        out_specs=row,
        out_shape=jax.ShapeDtypeStruct((t, d), f32),
        scratch_shapes=[pltpu.VMEM((2, G_TOK * TOP_K, SUBLANES, LANES), bf16),
                        pltpu.VMEM((2, G_TOK * PACKED_SUBLANES, LANES), f32),
                        pltpu.SemaphoreType.DMA((2,))],
        compiler_params=_cparams(1),
        name="combine_final" if final else "combine",
    )(dest_blk, wts_blk, cnt_tab, gst_tab, cnt_tab, gst_tab, hmid, mod, final_g, yg)


def _tok_major(a_t):
    return a_t.T.reshape(-1)


def kernel(x, c, ada_w, ada_b, norm1_g, w_in, conv_w, conv_b, rg_wa, rg_ba, rg_wx, rg_bx, rg_lambda,
           gla_w_up, gla_b_up, gla_norm_g, w_branch_a, w_branch_b, w_out, norm2_g, router_w, router_b,
           exp_w_gu, exp_w_down, shared_w_gu, shared_w_down, final_g):
    bsz, s, d = x.shape
    n_layers = ada_w.shape[0]
    t = bsz * s
    dqk = GLA_HEADS * GLA_DK
    n_assign = t * TOP_K
    nb = -(-(n_assign + N_EXPERTS * (R_EXP - 1)) // R_EXP)
    p = nb * R_EXP
    n_groups = t // G_TOK

    mod_all = _ada(c, ada_w, ada_b)
    h = x
    for l in range(n_layers):
        mod = mod_all[l].reshape(bsz, 1, 6 * d)
        wl = w_in[l]
        lr0 = 2 * d + 2 * dqk + 2 * d
        w_main = jnp.concatenate([wl[:, :lr0], wl[:, lr0 + GLA_RANK:]], axis=1).astype(bf16)
        w_lr = jnp.pad(wl[:, lr0:lr0 + GLA_RANK], ((0, 0), (0, LANES - GLA_RANK))).astype(bf16)
        w_up = jnp.pad(gla_w_up[l], ((0, LANES - GLA_RANK), (0, 0))).astype(bf16)
        xr, gg, q, k, v, og, ga, gb, la = _in_proj(
            h, mod, norm1_g[l].reshape(1, d), w_main, w_lr, w_up, gla_b_up[l].reshape(1, dqk))

        wab = jnp.concatenate([rg_wa[l], rg_wx[l]], axis=-1).astype(bf16)
        ya_in = _rglru(xr, gg, conv_w[l], conv_b[l].reshape(1, d), wab,
                       rg_ba[l].reshape(1, d), rg_bx[l].reshape(1, d), rg_lambda[l].reshape(1, d))
        ob = _gla(q, k, v, la, og, gla_norm_g[l].reshape(1, d))

        wr_hi = router_w[l].astype(bf16)
        wr_lo = (router_w[l] - wr_hi.astype(f32)).astype(bf16)
        hmid, urows, idx_t, wts_t, rank_t, counts = _post(
            ya_in, ob, ga, gb, h, mod,
            w_branch_a[l].astype(bf16), w_branch_b[l].astype(bf16), w_out[l].astype(bf16),
            norm2_g[l].reshape(1, d), shared_w_gu[l].astype(bf16), shared_w_down[l].astype(bf16),
            wr_hi, wr_lo, router_b[l].reshape(N_EXPERTS, 1))

        cnt = counts[:, :, 0].astype(jnp.int32)
        lstart = jnp.cumsum(cnt, axis=1) - cnt
        ctot = jnp.sum(cnt, axis=0)
        padded = (ctot + R_EXP - 1) // R_EXP * R_EXP
        ends = jnp.cumsum(padded)
        pstart = ends - padded
        gstart = pstart[None, :] + jnp.cumsum(cnt, axis=0) - cnt
        blk_row = jnp.arange(nb, dtype=jnp.int32) * R_EXP
        block_exp = jnp.minimum(
            jnp.sum((ends[None, :] <= blk_row[:, None]).astype(jnp.int32), axis=1),
            N_EXPERTS - 1).astype(jnp.int32)
        nused = (ends[-1:] // R_EXP).astype(jnp.int32)
        cnt_tab = cnt.reshape(n_groups, 1, N_EXPERTS)
        gst_tab = gstart.astype(jnp.int32).reshape(n_groups, 1, N_EXPERTS)

        dest_t = _dest(idx_t, rank_t, lstart.astype(f32).reshape(n_groups, N_EXPERTS, 1))
        dest_blk = _tok_major(dest_t)
        xg = _dispatch(dest_blk, cnt_tab, gst_tab,
                       (pstart + ctot).astype(jnp.int32).reshape(1, N_EXPERTS),
                       (padded - ctot).astype(jnp.int32).reshape(1, N_EXPERTS),
                       nused, urows.reshape(t, SUBLANES, LANES), p)
        yg = _experts(block_exp, nused, xg.reshape(p * SUBLANES, LANES), exp_w_gu, exp_w_down, l)
        h = _combine(dest_blk, _tok_major(wts_t), cnt_tab, gst_tab, hmid.reshape(t, d), mod,
                     final_g.reshape(1, d), yg.reshape(p, SUBLANES, LANES),
                     l == n_layers - 1).reshape(bsz, s, d)
    return h
```

```python
import functools

import jax
import jax.numpy as jnp
import numpy as np
from jax import lax
from jax.experimental import pallas as pl
from jax.experimental.pallas import tpu as pltpu

EPS = 1e-6
CHUNK = 64
CONV_W = 4
RG_C = 8.0
RNN_BLOCKS = 8
GLA_HEADS = 4
GLA_DK = 128
GLA_DV = 256
GLA_RANK = 16
GLA_TAU = 16.0
N_EXPERTS = 128
TOP_K = 8
D_EXPERT = 256
ROUTED_SCALE = 2.5

LANES = 128
SUBLANES = 8
PACKED_SUBLANES = 4
VMEM_LIMIT = 56 * 1024 * 1024

TM_IN = 512
TS_RNN = 512
TS_GLA = 512
TM_POST = 512
G_TOK = 1024
POST_SPLIT = 2
R_EXP = 1024
EXP_SPLIT = 2

bf16 = jnp.bfloat16
f32 = jnp.float32


def _cparams(n_axes):
    return pltpu.CompilerParams(
        dimension_semantics=("arbitrary",) * n_axes,
        vmem_limit_bytes=VMEM_LIMIT,
    )


def _dot(a, b):
    return jnp.dot(a, b, preferred_element_type=f32)


def _dot_nt(a, b):
    return lax.dot_general(a, b, (((1,), (1,)), ((), ())), preferred_element_type=f32)


def _dot_tn(a, b):
    return lax.dot_general(a, b, (((0,), (0,)), ((), ())), preferred_element_type=f32)


def _split_bf16(x):
    hi = x.astype(bf16)
    lo = (x - hi.astype(f32)).astype(bf16)
    return hi, lo


def _pack_bf16_pair(lo, hi):
    lo_bits = lax.bitcast_convert_type(lo.astype(f32), jnp.uint32) >> 16
    hi_bits = lax.bitcast_convert_type(hi.astype(f32), jnp.uint32) & jnp.uint32(0xFFFF0000)
    return lax.bitcast_convert_type(hi_bits | lo_bits, jnp.int32)


def _unpack_bf16_pair(word):
    bits = lax.bitcast_convert_type(word, jnp.uint32)
    lo = lax.bitcast_convert_type(bits << 16, f32).astype(bf16)
    hi = lax.bitcast_convert_type(bits & jnp.uint32(0xFFFF0000), f32).astype(bf16)
    return lo, hi


def _packed_rows_view(buf):
    two, n, sl, ln = buf.shape
    return (buf.reshape(two, n // 2, 2 * sl, ln).bitcast(jnp.int32)
            .reshape(two, n * PACKED_SUBLANES, ln))


def _rmsnorm_rows(x, g):
    return x * lax.rsqrt(jnp.mean(x * x, axis=-1, keepdims=True) + EPS) * g


def _log_sigmoid(x):
    return jnp.minimum(x, 0.0) - jnp.log1p(jnp.exp(-jnp.abs(x)))


def _softplus(x):
    return jnp.maximum(x, 0.0) + jnp.log1p(jnp.exp(-jnp.abs(x)))


def _ada_kernel(c_ref, w_ref, b_ref, o_ref):
    c_act = jax.nn.silu(c_ref[...])
    c_hi, c_lo = _split_bf16(c_act)
    w_hi, w_lo = _split_bf16(w_ref[...])
    acc = _dot(c_hi, w_hi) + _dot(c_lo, w_hi) + _dot(c_hi, w_lo)
    o_ref[...] = acc + b_ref[...]


def _ada(c, ada_w, ada_b):
    n_layers, d, n6 = ada_w.shape
    bsz = c.shape[0]
    tn = 512
    return pl.pallas_call(
        _ada_kernel,
        grid=(n_layers, n6 // tn),
        in_specs=[
            pl.BlockSpec((bsz, d), lambda l, j: (0, 0)),
            pl.BlockSpec((None, d, tn), lambda l, j: (l, 0, j)),
            pl.BlockSpec((None, 1, tn), lambda l, j: (l, 0, j)),
        ],
        out_specs=pl.BlockSpec((None, bsz, tn), lambda l, j: (l, 0, j)),
        out_shape=jax.ShapeDtypeStruct((n_layers, bsz, n6), f32),
        compiler_params=_cparams(2),
        name="ada_mod",
    )(c, ada_w, ada_b.reshape(n_layers, 1, n6))


def _in_kernel(h_ref, mod_ref, g_ref, w_ref, wlr_ref, wup_ref, bup_ref,
               xr_ref, gg_ref, q_ref, k_ref, v_ref, og_ref, ga_ref, gb_ref, la_ref, u_scr):
    d = h_ref.shape[-1]
    sh1 = mod_ref[:, 0:d]
    sc1 = mod_ref[:, d:2 * d]
    u = _rmsnorm_rows(h_ref[...], g_ref[...]) * (1.0 + sc1) + sh1
    u_scr[...] = u.astype(bf16)

    def proj(c0, width):
        return _dot(u_scr[...], w_ref[:, c0:c0 + width])

    cw = 512
    for c in range(d // cw):
        xr_ref[:, c * cw:(c + 1) * cw] = proj(c * cw, cw)
    for c in range(d // cw):
        y = proj(d + c * cw, cw)
        gg_ref[:, c * cw:(c + 1) * cw] = jax.nn.gelu(y, approximate=True).astype(bf16)
    dqk = GLA_HEADS * GLA_DK
    q_ref[...] = (proj(2 * d, dqk) * (GLA_DK ** -0.5)).astype(bf16)
    k_ref[...] = proj(2 * d + dqk, dqk).astype(bf16)
    base = 2 * d + 2 * dqk
    for c in range(d // cw):
        v_ref[:, c * cw:(c + 1) * cw] = proj(base + c * cw, cw).astype(bf16)
    base += d
    for c in range(d // cw):
        y = proj(base + c * cw, cw)
        og_ref[:, c * cw:(c + 1) * cw] = jax.nn.silu(y).astype(bf16)
    base += d
    for c in range(d // cw):
        y = proj(base + c * cw, cw)
        ga_ref[:, c * cw:(c + 1) * cw] = jax.nn.sigmoid(y).astype(bf16)
    base += d
    for c in range(d // cw):
        y = proj(base + c * cw, cw)
        gb_ref[:, c * cw:(c + 1) * cw] = jax.nn.sigmoid(y).astype(bf16)
    lr = _dot(u_scr[...], wlr_ref[...]).astype(bf16)
    logit = _dot(lr, wup_ref[...]) + bup_ref[...]
    la_ref[...] = _log_sigmoid(logit) * (1.0 / GLA_TAU)


def _in_proj(h, mod, norm_g, w_main, w_lr, w_up, b_up):
    bsz, s, d = h.shape
    tm = TM_IN
    dqk = GLA_HEADS * GLA_DK
    row = lambda width: pl.BlockSpec((None, tm, width), lambda b, i: (b, i, 0))
    full = lambda a: pl.BlockSpec(a.shape, lambda b, i: (0,) * a.ndim)
    out_shapes = [
        jax.ShapeDtypeStruct((bsz, s, d), f32),
        jax.ShapeDtypeStruct((bsz, s, d), bf16),
        jax.ShapeDtypeStruct((bsz, s, dqk), bf16),
        jax.ShapeDtypeStruct((bsz, s, dqk), bf16),
        jax.ShapeDtypeStruct((bsz, s, d), bf16),
        jax.ShapeDtypeStruct((bsz, s, d), bf16),
        jax.ShapeDtypeStruct((bsz, s, d), bf16),
        jax.ShapeDtypeStruct((bsz, s, d), bf16),
        jax.ShapeDtypeStruct((bsz, s, dqk), f32),
    ]
    return pl.pallas_call(
        _in_kernel,
        grid=(bsz, s // tm),
        in_specs=[
            row(d),
            pl.BlockSpec((None, 1, mod.shape[-1]), lambda b, i: (b, 0, 0)),
            full(norm_g),
            pl.BlockSpec(w_main.shape, lambda b, i: (0, 0), pipeline_mode=pl.Buffered(1)),
            full(w_lr), full(w_up), full(b_up),
        ],
        out_specs=[row(d), row(d), row(dqk), row(dqk), row(d), row(d), row(d), row(d), row(dqk)],
        out_shape=out_shapes,
        scratch_shapes=[pltpu.VMEM((tm, d), bf16)],
        compiler_params=_cparams(2),
        name="in_proj",
    )(h, mod, norm_g, w_main, w_lr, w_up, b_up)


def _rglru_kernel(xr_ref, gg_ref, cw_ref, cb_ref, wab_ref, ba_ref, bx_ref, lam_ref,
                  o_ref, xpad, hcar):
    ts, d = xr_ref.shape
    si = pl.program_id(1)

    @pl.when(si == 0)
    def _():
        xpad[0:SUBLANES, :] = jnp.zeros((SUBLANES, d), f32)
        hcar[...] = jnp.zeros_like(hcar)

    xpad[SUBLANES:SUBLANES + ts, :] = xr_ref[...]
    xc = cb_ref[...] + xpad[SUBLANES:SUBLANES + ts, :] * cw_ref[CONV_W - 1:CONV_W, :]
    for j in range(CONV_W - 1):
        off = SUBLANES - (CONV_W - 1) + j
        xc = xc + xpad[off:off + ts, :] * cw_ref[j:j + 1, :]
    xpad[0:SUBLANES, :] = xpad[ts:ts + SUBLANES, :]

    groups = ts // SUBLANES
    bw = d // RNN_BLOCKS
    sub = lax.broadcasted_iota(jnp.int32, (groups, SUBLANES, bw), 1)
    first_row = (lax.broadcasted_iota(jnp.int32, (ts, bw), 0) == 0) & (si == 0)
    neg_c_sp = (-RG_C) * _softplus(-lam_ref[...])
    xcb = xc.astype(bf16)
    for hb in range(RNN_BLOCKS):
        cs = slice(hb * bw, (hb + 1) * bw)
        x_h = xc[:, cs]
        g = _dot(xcb[:, cs], wab_ref[hb])
        r = jax.nn.sigmoid(g[:, :bw] + ba_ref[:, cs])
        i = jax.nn.sigmoid(g[:, bw:] + bx_ref[:, cs])
        log_a = r * neg_c_sp[:, cs]
        a = jnp.exp(log_a)
        mult = jnp.sqrt(1.0 - a * a)
        mult = jnp.where(first_row, 1.0, mult)
        bt = mult * (i * x_h)
        av = a.reshape(groups, SUBLANES, bw)
        bv = bt.reshape(groups, SUBLANES, bw)
        for sh in (1, 2, 4):
            a_prev = pltpu.roll(av, sh, 1)
            b_prev = pltpu.roll(bv, sh, 1)
            keep = sub >= sh
            bv = jnp.where(keep, av * b_prev + bv, bv)
            av = jnp.where(keep, av * a_prev, av)
        hin = hcar[0:1, cs]
        outs = []
        for gi in range(groups):
            hg = av[gi] * hin + bv[gi]
            outs.append(hg)
            hin = hg[SUBLANES - 1:SUBLANES, :]
        hcar[0:1, cs] = hin
        hs = jnp.concatenate(outs, axis=0)
        o_ref[:, cs] = (hs * gg_ref[:, cs].astype(f32)).astype(bf16)


def _rglru(xr, gg, conv_w, conv_b, wab, ba, bx, lam):
    bsz, s, d = xr.shape
    ts = TS_RNN
    row = pl.BlockSpec((None, ts, d), lambda b, i: (b, i, 0))
    full = lambda a: pl.BlockSpec(a.shape, lambda b, i: (0,) * a.ndim)
    return pl.pallas_call(
        _rglru_kernel,
        grid=(bsz, s // ts),
        in_specs=[row, row, full(conv_w), full(conv_b), full(wab), full(ba), full(bx), full(lam)],
        out_specs=row,
        out_shape=jax.ShapeDtypeStruct((bsz, s, d), bf16),
        scratch_shapes=[pltpu.VMEM((ts + SUBLANES, d), f32), pltpu.VMEM((SUBLANES, d), f32)],
        compiler_params=_cparams(2),
        name="rglru",
    )(xr, gg, conv_w, conv_b, wab, ba, bx, lam)


def _gla_kernel(q_ref, k_ref, v_ref, la_ref, og_ref, g_ref, o_ref, st, kd, oraw):
    ts = q_ref.shape[0]
    si = pl.program_id(1)

    @pl.when(si == 0)
    def _():
        st[...] = jnp.zeros_like(st)

    ri = lax.broadcasted_iota(jnp.int32, (CHUNK, CHUNK), 0)
    ci = lax.broadcasted_iota(jnp.int32, (CHUNK, CHUNK), 1)
    tri = jnp.where(ci <= ri, 1.0, 0.0).astype(bf16)
    decs = []
    for c in range(ts // CHUNK):
        rs = slice(c * CHUNK, (c + 1) * CHUNK)
        la_hi, la_lo = _split_bf16(la_ref[rs, :])
        cum = _dot(tri, la_hi) + _dot(tri, la_lo)
        tot = cum[CHUNK - 1:CHUNK, :]
        kd[rs, :] = (k_ref[rs, :].astype(f32) * jnp.exp(tot - cum)).astype(bf16)
        decs.append(jnp.exp(tot))
    for c in range(ts // CHUNK):
        rs = slice(c * CHUNK, (c + 1) * CHUNK)
        for hd in range(GLA_HEADS):
            ks = slice(hd * GLA_DK, (hd + 1) * GLA_DK)
            vs = slice(hd * GLA_DV, (hd + 1) * GLA_DV)
            s_new = st[hd] * decs[c][:, ks] + _dot_tn(v_ref[rs, vs], kd[rs, ks])
            st[hd] = s_new
            oraw[rs, vs] = _dot_nt(q_ref[rs, ks], s_new.astype(bf16))
    for hd in range(GLA_HEADS):
        vs = slice(hd * GLA_DV, (hd + 1) * GLA_DV)
        o = _rmsnorm_rows(oraw[:, vs], g_ref[:, vs])
        o_ref[:, vs] = (o * og_ref[:, vs].astype(f32)).astype(bf16)


def _gla(q, k, v, la, og, norm_g):
    bsz, s, dqk = q.shape
    d = v.shape[-1]
    ts = TS_GLA
    row = lambda width: pl.BlockSpec((None, ts, width), lambda b, i: (b, i, 0))
    return pl.pallas_call(
        _gla_kernel,
        grid=(bsz, s // ts),
        in_specs=[row(dqk), row(dqk), row(d), row(dqk), row(d),
                  pl.BlockSpec(norm_g.shape, lambda b, i: (0, 0))],
        out_specs=row(d),
        out_shape=jax.ShapeDtypeStruct((bsz, s, d), bf16),
        scratch_shapes=[pltpu.VMEM((GLA_HEADS, GLA_DV, GLA_DK), f32),
                        pltpu.VMEM((ts, dqk), bf16), pltpu.VMEM((ts, d), f32)],
        compiler_params=_cparams(2),
        name="gla",
    )(q, k, v, la, og, norm_g)


def _post_kernel(ya_ref, ob_ref, ga_ref, gb_ref, h_ref, mod_ref, wa_ref, wb_ref, wo_ref, g2_ref,
                 wsgu_ref, wsd_ref, wrh_ref, wrl_ref, rb_ref,
                 hmid_ref, urow_ref, idx_ref, wts_ref, rank_ref, cnt_ref, cnt, lg):
    tm, d = h_ref.shape

    @pl.when(pl.program_id(1) % (G_TOK // tm) == 0)
    def _():
        cnt[...] = jnp.zeros_like(cnt)

    sub = tm // POST_SPLIT
    g1 = mod_ref[:, 2 * d:3 * d]
    sh2 = mod_ref[:, 3 * d:4 * d]
    sc2 = mod_ref[:, 4 * d:5 * d]
    g2 = mod_ref[:, 5 * d:6 * d]
    n_e = wrh_ref.shape[1]
    e_iota = lax.broadcasted_iota(jnp.int32, (n_e, sub), 0).astype(f32)
    ri = lax.broadcasted_iota(jnp.int32, (sub, sub), 0)
    ci = lax.broadcasted_iota(jnp.int32, (sub, sub), 1)
    upper = jnp.where(ri <= ci, 1.0, 0.0).astype(bf16)
    running = cnt[...]
    for hf in range(POST_SPLIT):
        rs = slice(hf * sub, (hf + 1) * sub)
        ya = _dot(ya_ref[rs, :], wa_ref[...])
        yb = _dot(ob_ref[rs, :], wb_ref[...])
        m = (ga_ref[rs, :].astype(f32) * ya + gb_ref[rs, :].astype(f32) * yb).astype(bf16)
        h1 = h_ref[rs, :] + g1 * _dot(m, wo_ref[...])
        u2 = _rmsnorm_rows(h1, g2_ref[...]) * (1.0 + sc2) + sh2
        u_hi, u_lo = _split_bf16(u2)

        gu = _dot(u_hi, wsgu_ref[...])
        dsh = wsd_ref.shape[0]
        act = (jax.nn.silu(gu[:, :dsh]) * gu[:, dsh:]).astype(bf16)
        hmid_ref[rs, :] = h1 + g2 * _dot(act, wsd_ref[...])

        uw = urow_ref.bitcast(jnp.int32)
        for jp in range(d // LANES // 2):
            word = _pack_bf16_pair(u_hi[:, (2 * jp) * LANES:(2 * jp + 1) * LANES],
                                   u_hi[:, (2 * jp + 1) * LANES:(2 * jp + 2) * LANES])
            uw[pl.ds(hf * sub * PACKED_SUBLANES + jp, sub, stride=PACKED_SUBLANES), :] = word

        lg[:, rs] = (_dot(u_hi, wrh_ref[...]) + _dot(u_lo, wrh_ref[...])
                     + _dot(u_hi, wrl_ref[...])).T

    for hf in range(POST_SPLIT):
        rs = slice(hf * sub, (hf + 1) * sub)
        scores = jax.nn.sigmoid(lg[:, rs])
        work = scores + rb_ref[...]
        onehots, idxs, svals = [], [], []
        for _ in range(TOP_K):
            mx = jnp.max(work, axis=0, keepdims=True)
            ik = jnp.min(jnp.where(work == mx, e_iota, float(n_e)), axis=0, keepdims=True)
            oh = e_iota == ik
            onehots.append(oh)
            idxs.append(ik.astype(jnp.int32))
            svals.append(jnp.sum(jnp.where(oh, scores, 0.0), axis=0, keepdims=True))
            work = jnp.where(oh, -jnp.inf, work)
        ssum = svals[0]
        for sv in svals[1:]:
            ssum = ssum + sv
        idx_ref[:, rs] = jnp.concatenate(idxs, axis=0)
        wts_ref[:, rs] = jnp.concatenate([sv / ssum * ROUTED_SCALE for sv in svals], axis=0)

        member = jnp.where(onehots[0], 1.0, 0.0)
        for oh in onehots[1:]:
            member = member + jnp.where(oh, 1.0, 0.0)
        incl = _dot(member.astype(bf16), upper)
        base = jnp.concatenate([running] * (sub // LANES), axis=1)
        pos = base + incl - member
        ranks = [jnp.sum(jnp.where(oh, pos, 0.0), axis=0, keepdims=True) for oh in onehots]
        rank_ref[:, rs] = jnp.concatenate(ranks, axis=0).astype(jnp.int32)
        running = running + jnp.broadcast_to(incl[:, sub - 1:sub], running.shape)
    cnt[...] = running
    cnt_ref[...] = running


def _post(ya_in, ob, ga, gb, h, mod, wa, wb, wo, g2, wsgu, wsd, wr_hi, wr_lo, rb):
    bsz, s, d = h.shape
    tm = TM_POST
    t = bsz * s
    nsb = s // tm
    tpg = G_TOK // tm
    n_groups = t // G_TOK
    row = pl.BlockSpec((None, tm, d), lambda b, i: (b, i, 0))
    full = lambda a: pl.BlockSpec(a.shape, lambda b, i: (0,) * a.ndim)
    tok = pl.BlockSpec((TOP_K, tm), lambda b, i: (0, b * nsb + i))
    return pl.pallas_call(
        _post_kernel,
        grid=(bsz, nsb),
        in_specs=[row, row, row, row, row,
                  pl.BlockSpec((None, 1, mod.shape[-1]), lambda b, i: (b, 0, 0)),
                  full(wa), full(wb), full(wo), full(g2), full(wsgu), full(wsd),
                  full(wr_hi), full(wr_lo), full(rb)],
        out_specs=[row,
                   pl.BlockSpec((tm * SUBLANES, LANES), lambda b, i: (b * nsb + i, 0)),
                   tok, tok, tok,
                   pl.BlockSpec((None, N_EXPERTS, LANES),
                                lambda b, i: ((b * nsb + i) // tpg, 0, 0))],
        out_shape=[jax.ShapeDtypeStruct((bsz, s, d), f32),
                   jax.ShapeDtypeStruct((t * SUBLANES, LANES), bf16),
                   jax.ShapeDtypeStruct((TOP_K, t), jnp.int32),
                   jax.ShapeDtypeStruct((TOP_K, t), f32),
                   jax.ShapeDtypeStruct((TOP_K, t), jnp.int32),
                   jax.ShapeDtypeStruct((n_groups, N_EXPERTS, LANES), f32)],
        scratch_shapes=[pltpu.VMEM((N_EXPERTS, LANES), f32), pltpu.VMEM((N_EXPERTS, tm), f32)],
        compiler_params=_cparams(2),
        name="post",
    )(ya_in, ob, ga, gb, h, mod, wa, wb, wo, g2, wsgu, wsd, wr_hi, wr_lo, rb)


def _dest_kernel(idx_ref, rank_ref, pstart_ref, o_ref):
    tn = idx_ref.shape[1]
    n_e = pstart_ref.shape[0]
    e_iota = lax.broadcasted_iota(jnp.int32, (n_e, tn), 0)
    rows = []
    for k in range(TOP_K):
        start = jnp.sum(jnp.where(e_iota == idx_ref[k:k + 1, :], pstart_ref[...], 0.0),
                        axis=0, keepdims=True)
        rows.append(start.astype(jnp.int32) + rank_ref[k:k + 1, :])
    o_ref[...] = jnp.concatenate(rows, axis=0)


def _dest(idx_t, rank_t, lstart_col):
    t = idx_t.shape[1]
    tn = G_TOK
    tok = pl.BlockSpec((TOP_K, tn), lambda i: (0, i))
    return pl.pallas_call(
        _dest_kernel,
        grid=(t // tn,),
        in_specs=[tok, tok, pl.BlockSpec((None, N_EXPERTS, 1), lambda i: (i, 0, 0))],
        out_specs=tok,
        out_shape=jax.ShapeDtypeStruct((TOP_K, t), jnp.int32),
        compiler_params=_cparams(1),
        name="dest",
    )(idx_t, rank_t, lstart_col)


TOK_UNROLL = 16


def _table_entry(tab_ref, i, c):
    return tab_ref[i * (TOK_UNROLL * TOP_K) + c]


def _disp_kernel(dest_ref, cnt_ref, gst_ref, pads_ref, padn_ref, nused_ref, u_ref, xg_ref,
                 xloc, zbuf, sem):
    g_tok = u_ref.shape[0]
    n_rows = g_tok * TOP_K
    g = pl.program_id(0)
    last = pl.num_programs(0) - 1
    slot = g % 2

    def drain(s):
        pltpu.make_async_copy(xloc.at[s], xg_ref.at[pl.ds(0, n_rows)], sem.at[s]).wait()

    @pl.when(g == 0)
    def _():
        zbuf[...] = jnp.zeros_like(zbuf)

    @pl.when(g >= 2)
    def _():
        drain(slot)

    xs = xloc.at[slot]
    uw = u_ref.bitcast(jnp.int32)
    xw2 = _packed_rows_view(xloc).at[slot]

    def fill(i, carry):
        for uu in range(TOK_UNROLL):
            t = i * TOK_UNROLL + uu
            row = uw[t]
            for k in range(TOP_K):
                slot_row = _table_entry(dest_ref, i, uu * TOP_K + k)
                off = pl.multiple_of(slot_row * PACKED_SUBLANES, PACKED_SUBLANES)
                xw2[pl.ds(off, PACKED_SUBLANES), :] = row
        return carry

    lax.fori_loop(0, g_tok // TOK_UNROLL, fill, 0)

    def seg(e, ls):
        c = cnt_ref[0, e]

        @pl.when(c > 0)
        def _():
            pltpu.make_async_copy(xs.at[pl.ds(ls, c)], xg_ref.at[pl.ds(gst_ref[0, e], c)],
                                  sem.at[slot]).start()

        return ls + c

    lax.fori_loop(0, N_EXPERTS, seg, 0)

    @pl.when(g == last)
    def _():
        def pad_copy(e):
            n = padn_ref[0, e]
            return pltpu.make_async_copy(zbuf.at[pl.ds(0, n)], xg_ref.at[pl.ds(pads_ref[0, e], n)],
                                         sem.at[2])

        def pad_start(e, carry):
            @pl.when(padn_ref[0, e] > 0)
            def _():
                pad_copy(e).start()
            return carry

        def pad_wait(e, carry):
            @pl.when(padn_ref[0, e] > 0)
            def _():
                pad_copy(e).wait()
            return carry

        n_blocks = xg_ref.shape[0] // R_EXP

        def tail_copy(i):
            return pltpu.make_async_copy(
                zbuf, xg_ref.at[pl.ds(pl.multiple_of(i * R_EXP, R_EXP), R_EXP)], sem.at[2])

        def tail_start(i, carry):
            tail_copy(i).start()
            return carry

        def tail_wait(i, carry):
            tail_copy(i).wait()
            return carry

        lax.fori_loop(0, N_EXPERTS, pad_start, 0)
        lax.fori_loop(nused_ref[0, 0], n_blocks, tail_start, 0)
        lax.fori_loop(0, N_EXPERTS, pad_wait, 0)
        lax.fori_loop(nused_ref[0, 0], n_blocks, tail_wait, 0)
        drain(1 - slot)
        drain(slot)


def _dispatch(dest_blk, cnt_tab, gst_tab, pad_start, pad_len, nused, urows, p_rows):
    t = urows.shape[0]
    n_groups = t // G_TOK
    assert n_groups >= 2
    smem_grp = lambda: pl.BlockSpec((None, 1, N_EXPERTS), lambda i: (i, 0, 0),
                                    memory_space=pltpu.SMEM)
    smem_all = lambda: pl.BlockSpec((1, N_EXPERTS), lambda i: (0, 0), memory_space=pltpu.SMEM)
    return pl.pallas_call(
        _disp_kernel,
        grid=(n_groups,),
        in_specs=[
            pl.BlockSpec((G_TOK * TOP_K,), lambda i: (i,), memory_space=pltpu.SMEM),
            smem_grp(), smem_grp(), smem_all(), smem_all(),
            pl.BlockSpec((1, 1), lambda i: (0, 0), memory_space=pltpu.SMEM),
            pl.BlockSpec((G_TOK, SUBLANES, LANES), lambda i: (i, 0, 0)),
        ],
        out_specs=pl.BlockSpec(memory_space=pl.ANY),
        out_shape=jax.ShapeDtypeStruct((p_rows, SUBLANES, LANES), bf16),
        scratch_shapes=[pltpu.VMEM((2, G_TOK * TOP_K, SUBLANES, LANES), bf16),
                        pltpu.VMEM((R_EXP, SUBLANES, LANES), bf16),
                        pltpu.SemaphoreType.DMA((3,))],
        compiler_params=_cparams(1),
        name="dispatch",
    )(dest_blk, cnt_tab, gst_tab, pad_start, pad_len, nused.reshape(1, 1), urows)


def _experts_kernel(be_ref, nused_ref, x_ref, wgu_ref, wd_ref, y_ref, wgu_b, wd_b):
    rows = x_ref.shape[0] // SUBLANES
    d = wgu_ref.shape[0]
    i = pl.program_id(0)

    @pl.when((i == 0) | (be_ref[i] != be_ref[jnp.maximum(i - 1, 0)]))
    def _():
        wgu_b[...] = wgu_ref[...].astype(bf16)
        wd_b[...] = wd_ref[...].astype(bf16)

    @pl.when(i < nused_ref[0])
    def _():
        xw = x_ref.bitcast(jnp.int32)
        yw = y_ref.bitcast(jnp.int32)
        sub = rows // EXP_SPLIT
        for hf in range(EXP_SPLIT):
            base = hf * sub * PACKED_SUBLANES
            gu = None
            for jp in range(d // LANES // 2):
                lo, hi = _unpack_bf16_pair(xw[pl.ds(base + jp, sub, stride=PACKED_SUBLANES), :])
                part = _dot(jnp.concatenate([lo, hi], axis=1),
                            wgu_b[2 * jp * LANES:(2 * jp + 2) * LANES, :])
                gu = part if gu is None else gu + part
            de = wd_b.shape[0]
            act = (jax.nn.silu(gu[:, :de]) * gu[:, de:]).astype(bf16)
            y = _dot(act, wd_b[...]).astype(bf16)
            for jp in range(d // LANES // 2):
                yw[pl.ds(base + jp, sub, stride=PACKED_SUBLANES), :] = _pack_bf16_pair(
                    y[:, (2 * jp) * LANES:(2 * jp + 1) * LANES],
                    y[:, (2 * jp + 1) * LANES:(2 * jp + 2) * LANES])

    @pl.when(i >= nused_ref[0])
    def _():
        y_ref[...] = jnp.zeros_like(y_ref)


def _experts(block_exp, nused, xg2, w_gu, w_down, layer):
    p8 = xg2.shape[0]
    _, n_e, d, de2 = w_gu.shape
    rb = R_EXP * SUBLANES
    nb = p8 // rb
    grid_spec = pltpu.PrefetchScalarGridSpec(
        num_scalar_prefetch=2,
        grid=(nb,),
        in_specs=[
            pl.BlockSpec((rb, LANES), lambda i, be, nu: (jnp.minimum(i, nu[0] - 1), 0)),
            pl.BlockSpec((None, None, d, de2), lambda i, be, nu: (layer, be[i], 0, 0)),
            pl.BlockSpec((None, None, de2 // 2, d), lambda i, be, nu: (layer, be[i], 0, 0)),
        ],
        out_specs=pl.BlockSpec((rb, LANES), lambda i, be, nu: (i, 0)),
        scratch_shapes=[pltpu.VMEM((d, de2), bf16), pltpu.VMEM((de2 // 2, d), bf16)],
    )
    return pl.pallas_call(
        _experts_kernel,
        grid_spec=grid_spec,
        out_shape=jax.ShapeDtypeStruct((p8, LANES), bf16),
        compiler_params=_cparams(1),
        name="experts",
    )(block_exp, nused, xg2, w_gu, w_down)


def _comb_kernel(dest_ref, wts_ref, cnt_ref, gst_ref, cntn_ref, gstn_ref, hmid_ref, mod_ref, fg_ref,
                 yg_ref, o_ref, yloc, acc, sem, *, final):
    g_tok, d = hmid_ref.shape
    n_rows = g_tok * TOP_K
    g = pl.program_id(0)
    n_groups = pl.num_programs(0)
    slot = g % 2

    def fetch(cnt_r, gst_r, s):
        def seg(e, ls):
            c = cnt_r[0, e]

            @pl.when(c > 0)
            def _():
                pltpu.make_async_copy(yg_ref.at[pl.ds(gst_r[0, e], c)],
                                      yloc.at[s, pl.ds(ls, c)], sem.at[s]).start()

            return ls + c

        lax.fori_loop(0, N_EXPERTS, seg, 0)

    @pl.when(g == 0)
    def _():
        fetch(cnt_ref, gst_ref, 0)

    @pl.when(g + 1 < n_groups)
    def _():
        fetch(cntn_ref, gstn_ref, 1 - slot)

    pltpu.make_async_copy(yg_ref.at[pl.ds(0, n_rows)], yloc.at[slot], sem.at[slot]).wait()

    yw2 = _packed_rows_view(yloc).at[slot]
    hi_mask = jnp.uint32(0xFFFF0000)

    def reduce(i, carry):
        for uu in range(TOK_UNROLL):
            t = i * TOK_UNROLL + uu
            a_lo = a_hi = None
            for k in range(TOP_K):
                slot_row = _table_entry(dest_ref, i, uu * TOP_K + k)
                off = pl.multiple_of(slot_row * PACKED_SUBLANES, PACKED_SUBLANES)
                bits = lax.bitcast_convert_type(yw2[pl.ds(off, PACKED_SUBLANES), :], jnp.uint32)
                w = _table_entry(wts_ref, i, uu * TOP_K + k)
                t_lo = lax.bitcast_convert_type(bits << 16, f32) * w
                t_hi = lax.bitcast_convert_type(bits & hi_mask, f32) * w
                a_lo = t_lo if a_lo is None else a_lo + t_lo
                a_hi = t_hi if a_hi is None else a_hi + t_hi
            row0 = pl.multiple_of(t * PACKED_SUBLANES, PACKED_SUBLANES)
            acc[0, pl.ds(row0, PACKED_SUBLANES), :] = a_lo
            acc[1, pl.ds(row0, PACKED_SUBLANES), :] = a_hi
        return carry

    lax.fori_loop(0, g_tok // TOK_UNROLL, reduce, 0)
    for j in range(d // LANES):
        cs = slice(j * LANES, (j + 1) * LANES)
        o_ref[:, cs] = hmid_ref[:, cs] + mod_ref[:, 5 * d + j * LANES:5 * d + (j + 1) * LANES] * \
            acc[j % 2, pl.ds(j // 2, g_tok, stride=PACKED_SUBLANES), :]
    if final:
        o_ref[...] = _rmsnorm_rows(o_ref[...], fg_ref[...])


def _combine(dest_blk, wts_blk, cnt_tab, gst_tab, hmid, mod, final_g, yg, final):
    t, d = hmid.shape
    n_groups = t // G_TOK
    gpb = n_groups // mod.shape[0]
    row = pl.BlockSpec((G_TOK, d), lambda i: (i, 0))
    smem_tok = lambda: pl.BlockSpec((G_TOK * TOP_K,), lambda i: (i,), memory_space=pltpu.SMEM)
    smem_cur = lambda: pl.BlockSpec((None, 1, N_EXPERTS), lambda i: (i, 0, 0),
                                    memory_space=pltpu.SMEM)
    smem_nxt = lambda: pl.BlockSpec((None, 1, N_EXPERTS),
                                    lambda i: (jnp.minimum(i + 1, n_groups - 1), 0, 0),
                                    memory_space=pltpu.SMEM)
    return pl.pallas_call(
        functools.partial(_comb_kernel, final=final),
        grid=(n_groups,),
        in_specs=[smem_tok(), smem_tok(), smem_cur(), smem_cur(), smem_nxt(), smem_nxt(), row,
                  pl.BlockSpec((None, 1, mod.shape[-1]), lambda i: (i // gpb, 0, 0)),
                  pl.BlockSpec(final_g.shape, lambda i: (0, 0)),
                  pl.BlockSpec(memory_space=pl.ANY)],
        out_specs=row,
        out_shape=jax.ShapeDtypeStruct((t, d), f32),
        scratch_shapes=[pltpu.VMEM((2, G_TOK * TOP_K, SUBLANES, LANES), bf16),
                        pltpu.VMEM((2, G_TOK * PACKED_SUBLANES, LANES), f32),
                        pltpu.SemaphoreType.DMA((2,))],
        compiler_params=_cparams(1),
        name="combine_final" if final else "combine",
    )(dest_blk, wts_blk, cnt_tab, gst_tab, cnt_tab, gst_tab, hmid, mod, final_g, yg)


def _tok_major(a_t):
    return a_t.T.reshape(-1)


def kernel(x, c, ada_w, ada_b, norm1_g, w_in, conv_w, conv_b, rg_wa, rg_ba, rg_wx, rg_bx, rg_lambda,
           gla_w_up, gla_b_up, gla_norm_g, w_branch_a, w_branch_b, w_out, norm2_g, router_w, router_b,
           exp_w_gu, exp_w_down, shared_w_gu, shared_w_down, final_g):
    bsz, s, d = x.shape
    n_layers = ada_w.shape[0]
    t = bsz * s
    dqk = GLA_HEADS * GLA_DK
    n_assign = t * TOP_K
    nb = -(-(n_assign + N_EXPERTS * (R_EXP - 1)) // R_EXP)
    p = nb * R_EXP
    n_groups = t // G_TOK

    mod_all = _ada(c, ada_w, ada_b)
    h = x
    for l in range(n_layers):
        mod = mod_all[l].reshape(bsz, 1, 6 * d)
        wl = w_in[l]
        lr0 = 2 * d + 2 * dqk + 2 * d
        w_main = jnp.concatenate([wl[:, :lr0], wl[:, lr0 + GLA_RANK:]], axis=1).astype(bf16)
        w_lr = jnp.pad(wl[:, lr0:lr0 + GLA_RANK], ((0, 0), (0, LANES - GLA_RANK))).astype(bf16)
        w_up = jnp.pad(gla_w_up[l], ((0, LANES - GLA_RANK), (0, 0))).astype(bf16)
        xr, gg, q, k, v, og, ga, gb, la = _in_proj(
            h, mod, norm1_g[l].reshape(1, d), w_main, w_lr, w_up, gla_b_up[l].reshape(1, dqk))

        wab = jnp.concatenate([rg_wa[l], rg_wx[l]], axis=-1).astype(bf16)
        ya_in = _rglru(xr, gg, conv_w[l], conv_b[l].reshape(1, d), wab,
                       rg_ba[l].reshape(1, d), rg_bx[l].reshape(1, d), rg_lambda[l].reshape(1, d))
        ob = _gla(q, k, v, la, og, gla_norm_g[l].reshape(1, d))

        wr_hi = router_w[l].astype(bf16)
        wr_lo = (router_w[l] - wr_hi.astype(f32)).astype(bf16)
        hmid, urows, idx_t, wts_t, rank_t, counts = _post(
            ya_in, ob, ga, gb, h, mod,
            w_branch_a[l].astype(bf16), w_branch_b[l].astype(bf16), w_out[l].astype(bf16),
            norm2_g[l].reshape(1, d), shared_w_gu[l].astype(bf16), shared_w_down[l].astype(bf16),
            wr_hi, wr_lo, router_b[l].reshape(N_EXPERTS, 1))

        cnt = counts[:, :, 0].astype(jnp.int32)
        lstart = jnp.cumsum(cnt, axis=1) - cnt
        ctot = jnp.sum(cnt, axis=0)
        padded = (ctot + R_EXP - 1) // R_EXP * R_EXP
        ends = jnp.cumsum(padded)
        pstart = ends - padded
        gstart = pstart[None, :] + jnp.cumsum(cnt, axis=0) - cnt
        blk_row = jnp.arange(nb, dtype=jnp.int32) * R_EXP
        block_exp = jnp.minimum(
            jnp.sum((ends[None, :] <= blk_row[:, None]).astype(jnp.int32), axis=1),
            N_EXPERTS - 1).astype(jnp.int32)
        nused = (ends[-1:] // R_EXP).astype(jnp.int32)
        cnt_tab = cnt.reshape(n_groups, 1, N_EXPERTS)
        gst_tab = gstart.astype(jnp.int32).reshape(n_groups, 1, N_EXPERTS)

        dest_t = _dest(idx_t, rank_t, lstart.astype(f32).reshape(n_groups, N_EXPERTS, 1))
        dest_blk = _tok_major(dest_t)
        xg = _dispatch(dest_blk, cnt_tab, gst_tab,
                       (pstart + ctot).astype(jnp.int32).reshape(1, N_EXPERTS),
                       (padded - ctot).astype(jnp.int32).reshape(1, N_EXPERTS),
                       nused, urows.reshape(t, SUBLANES, LANES), p)
        yg = _experts(block_exp, nused, xg.reshape(p * SUBLANES, LANES), exp_w_gu, exp_w_down, l)
        h = _combine(dest_blk, _tok_major(wts_t), cnt_tab, gst_tab, hmid.reshape(t, d), mod,
                     final_g.reshape(1, d), yg.reshape(p, SUBLANES, LANES),
                     l == n_layers - 1).reshape(bsz, s, d)
    return h
```

```python
import functools

import jax
import jax.numpy as jnp
from jax import lax
from jax.experimental import pallas as pl
from jax.experimental.pallas import tpu as pltpu

EPS = 1e-6
CHUNK = 64
CONV_W = 4
RG_C = 8.0
RNN_BLOCKS = 8
GLA_HEADS = 4
GLA_DK = 128
GLA_DV = 256
GLA_RANK = 16
GLA_TAU = 16.0
N_EXPERTS = 128
TOP_K = 8
D_EXPERT = 256
ROUTED_SCALE = 2.5

LANES = 128
SUBLANES = 8
PACKED_SUBLANES = 4
VMEM_LIMIT = 56 * 1024 * 1024

TM_IN = 512
TS_RNN = 512
TS_GLA = 512
TM_POST = 512
G_TOK = 1024
POST_SPLIT = 2
R_EXP = 1024
EXP_SPLIT = 2

bf16 = jnp.bfloat16
f32 = jnp.float32


def _cparams(n_axes):
    return pltpu.CompilerParams(
        dimension_semantics=("arbitrary",) * n_axes,
        vmem_limit_bytes=VMEM_LIMIT,
    )


def _dot(a, b):
    return jnp.dot(a, b, preferred_element_type=f32)


def _dot_nt(a, b):
    return lax.dot_general(a, b, (((1,), (1,)), ((), ())), preferred_element_type=f32)


def _dot_tn(a, b):
    return lax.dot_general(a, b, (((0,), (0,)), ((), ())), preferred_element_type=f32)


def _split_bf16(x):
    hi = x.astype(bf16)
    lo = (x - hi.astype(f32)).astype(bf16)
    return hi, lo


def _pack_bf16_pair(lo, hi):
    lo_bits = lax.bitcast_convert_type(lo.astype(f32), jnp.uint32) >> 16
    hi_bits = lax.bitcast_convert_type(hi.astype(f32), jnp.uint32) & jnp.uint32(0xFFFF0000)
    return lax.bitcast_convert_type(hi_bits | lo_bits, jnp.int32)


def _unpack_bf16_pair(word):
    bits = lax.bitcast_convert_type(word, jnp.uint32)
    lo = lax.bitcast_convert_type(bits << 16, f32).astype(bf16)
    hi = lax.bitcast_convert_type(bits & jnp.uint32(0xFFFF0000), f32).astype(bf16)
    return lo, hi


def _packed_rows_view(buf):
    two, n, sl, ln = buf.shape
    return (buf.reshape(two, n // 2, 2 * sl, ln).bitcast(jnp.int32)
            .reshape(two, n * PACKED_SUBLANES, ln))


def _rmsnorm_rows(x, g):
    return x * lax.rsqrt(jnp.mean(x * x, axis=-1, keepdims=True) + EPS) * g


def _log_sigmoid(x):
    return jnp.minimum(x, 0.0) - jnp.log1p(jnp.exp(-jnp.abs(x)))


def _softplus(x):
    return jnp.maximum(x, 0.0) + jnp.log1p(jnp.exp(-jnp.abs(x)))


def _ada_kernel(c_ref, w_ref, b_ref, o_ref):
    c_act = jax.nn.silu(c_ref[...])
    c_hi, c_lo = _split_bf16(c_act)
    w_hi, w_lo = _split_bf16(w_ref[...])
    acc = _dot(c_hi, w_hi) + _dot(c_lo, w_hi) + _dot(c_hi, w_lo)
    o_ref[...] = acc + b_ref[...]


def _ada(c, ada_w, ada_b):
    n_layers, d, n6 = ada_w.shape
    bsz = c.shape[0]
    tn = 512
    return pl.pallas_call(
        _ada_kernel,
        grid=(n_layers, n6 // tn),
        in_specs=[
            pl.BlockSpec((bsz, d), lambda l, j: (0, 0)),
            pl.BlockSpec((None, d, tn), lambda l, j: (l, 0, j)),
            pl.BlockSpec((None, 1, tn), lambda l, j: (l, 0, j)),
        ],
        out_specs=pl.BlockSpec((None, bsz, tn), lambda l, j: (l, 0, j)),
        out_shape=jax.ShapeDtypeStruct((n_layers, bsz, n6), f32),
        compiler_params=_cparams(2),
        name="ada_mod",
    )(c, ada_w, ada_b.reshape(n_layers, 1, n6))


def _in_kernel(h_ref, mod_ref, g_ref, w_ref, wlr_ref, wup_ref, bup_ref,
               xr_ref, gg_ref, q_ref, k_ref, v_ref, og_ref, ga_ref, gb_ref, la_ref, u_scr):
    d = h_ref.shape[-1]
    sh1 = mod_ref[:, 0:d]
    sc1 = mod_ref[:, d:2 * d]
    u = _rmsnorm_rows(h_ref[...], g_ref[...]) * (1.0 + sc1) + sh1
    u_scr[...] = u.astype(bf16)

    def proj(c0, width):
        return _dot(u_scr[...], w_ref[:, c0:c0 + width])

    cw = 512
    for c in range(d // cw):
        xr_ref[:, c * cw:(c + 1) * cw] = proj(c * cw, cw)
    for c in range(d // cw):
        y = proj(d + c * cw, cw)
        gg_ref[:, c * cw:(c + 1) * cw] = jax.nn.gelu(y, approximate=True).astype(bf16)
    dqk = GLA_HEADS * GLA_DK
    q_ref[...] = (proj(2 * d, dqk) * (GLA_DK ** -0.5)).astype(bf16)
    k_ref[...] = proj(2 * d + dqk, dqk).astype(bf16)
    base = 2 * d + 2 * dqk
    for c in range(d // cw):
        v_ref[:, c * cw:(c + 1) * cw] = proj(base + c * cw, cw).astype(bf16)
    base += d
    for c in range(d // cw):
        y = proj(base + c * cw, cw)
        og_ref[:, c * cw:(c + 1) * cw] = jax.nn.silu(y).astype(bf16)
    base += d
    for c in range(d // cw):
        y = proj(base + c * cw, cw)
        ga_ref[:, c * cw:(c + 1) * cw] = jax.nn.sigmoid(y).astype(bf16)
    base += d
    for c in range(d // cw):
        y = proj(base + c * cw, cw)
        gb_ref[:, c * cw:(c + 1) * cw] = jax.nn.sigmoid(y).astype(bf16)
    lr = _dot(u_scr[...], wlr_ref[...]).astype(bf16)
    logit = _dot(lr, wup_ref[...]) + bup_ref[...]
    la_ref[...] = _log_sigmoid(logit) * (1.0 / GLA_TAU)


def _in_proj(h, mod, norm_g, w_main, w_lr, w_up, b_up):
    bsz, s, d = h.shape
    tm = TM_IN
    dqk = GLA_HEADS * GLA_DK
    row = lambda width: pl.BlockSpec((None, tm, width), lambda b, i: (b, i, 0))
    full = lambda a: pl.BlockSpec(a.shape, lambda b, i: (0,) * a.ndim)
    out_shapes = [
        jax.ShapeDtypeStruct((bsz, s, d), f32),
        jax.ShapeDtypeStruct((bsz, s, d), bf16),
        jax.ShapeDtypeStruct((bsz, s, dqk), bf16),
        jax.ShapeDtypeStruct((bsz, s, dqk), bf16),
        jax.ShapeDtypeStruct((bsz, s, d), bf16),
        jax.ShapeDtypeStruct((bsz, s, d), bf16),
        jax.ShapeDtypeStruct((bsz, s, d), bf16),
        jax.ShapeDtypeStruct((bsz, s, d), bf16),
        jax.ShapeDtypeStruct((bsz, s, dqk), f32),
    ]
    return pl.pallas_call(
        _in_kernel,
        grid=(bsz, s // tm),
        in_specs=[
            row(d),
            pl.BlockSpec((None, 1, mod.shape[-1]), lambda b, i: (b, 0, 0)),
            full(norm_g),
            pl.BlockSpec(w_main.shape, lambda b, i: (0, 0), pipeline_mode=pl.Buffered(1)),
            full(w_lr), full(w_up), full(b_up),
        ],
        out_specs=[row(d), row(d), row(dqk), row(dqk), row(d), row(d), row(d), row(d), row(dqk)],
        out_shape=out_shapes,
        scratch_shapes=[pltpu.VMEM((tm, d), bf16)],
        compiler_params=_cparams(2),
        name="in_proj",
    )(h, mod, norm_g, w_main, w_lr, w_up, b_up)


def _rglru_kernel(xr_ref, gg_ref, cw_ref, cb_ref, wab_ref, ba_ref, bx_ref, lam_ref,
                  o_ref, xpad, hcar):
    ts, d = xr_ref.shape
    si = pl.program_id(1)

    @pl.when(si == 0)
    def _():
        xpad[0:SUBLANES, :] = jnp.zeros((SUBLANES, d), f32)
        hcar[...] = jnp.zeros_like(hcar)

    xpad[SUBLANES:SUBLANES + ts, :] = xr_ref[...]
    xc = cb_ref[...] + xpad[SUBLANES:SUBLANES + ts, :] * cw_ref[CONV_W - 1:CONV_W, :]
    for j in range(CONV_W - 1):
        off = SUBLANES - (CONV_W - 1) + j
        xc = xc + xpad[off:off + ts, :] * cw_ref[j:j + 1, :]
    xpad[0:SUBLANES, :] = xpad[ts:ts + SUBLANES, :]

    groups = ts // SUBLANES
    bw = d // RNN_BLOCKS
    sub = lax.broadcasted_iota(jnp.int32, (groups, SUBLANES, bw), 1)
    first_row = (lax.broadcasted_iota(jnp.int32, (ts, bw), 0) == 0) & (si == 0)
    neg_c_sp = (-RG_C) * _softplus(-lam_ref[...])
    xcb = xc.astype(bf16)
    for hb in range(RNN_BLOCKS):
        cs = slice(hb * bw, (hb + 1) * bw)
        x_h = xc[:, cs]
        g = _dot(xcb[:, cs], wab_ref[hb])
        r = jax.nn.sigmoid(g[:, :bw] + ba_ref[:, cs])
        i = jax.nn.sigmoid(g[:, bw:] + bx_ref[:, cs])
        log_a = r * neg_c_sp[:, cs]
        a = jnp.exp(log_a)
        mult = jnp.sqrt(1.0 - a * a)
        mult = jnp.where(first_row, 1.0, mult)
        bt = mult * (i * x_h)
        av = a.reshape(groups, SUBLANES, bw)
        bv = bt.reshape(groups, SUBLANES, bw)
        for sh in (1, 2, 4):
            a_prev = pltpu.roll(av, sh, 1)
            b_prev = pltpu.roll(bv, sh, 1)
            keep = sub >= sh
            bv = jnp.where(keep, av * b_prev + bv, bv)
            av = jnp.where(keep, av * a_prev, av)
        hin = hcar[0:1, cs]
        outs = []
        for gi in range(groups):
            hg = av[gi] * hin + bv[gi]
            outs.append(hg)
            hin = hg[SUBLANES - 1:SUBLANES, :]
        hcar[0:1, cs] = hin
        hs = jnp.concatenate(outs, axis=0)
        o_ref[:, cs] = (hs * gg_ref[:, cs].astype(f32)).astype(bf16)


def _rglru(xr, gg, conv_w, conv_b, wab, ba, bx, lam):
    bsz, s, d = xr.shape
    ts = TS_RNN
    row = pl.BlockSpec((None, ts, d), lambda b, i: (b, i, 0))
    full = lambda a: pl.BlockSpec(a.shape, lambda b, i: (0,) * a.ndim)
    return pl.pallas_call(
        _rglru_kernel,
        grid=(bsz, s // ts),
        in_specs=[row, row, full(conv_w), full(conv_b), full(wab), full(ba), full(bx), full(lam)],
        out_specs=row,
        out_shape=jax.ShapeDtypeStruct((bsz, s, d), bf16),
        scratch_shapes=[pltpu.VMEM((ts + SUBLANES, d), f32), pltpu.VMEM((SUBLANES, d), f32)],
        compiler_params=_cparams(2),
        name="rglru",
    )(xr, gg, conv_w, conv_b, wab, ba, bx, lam)


def _gla_kernel(q_ref, k_ref, v_ref, la_ref, og_ref, g_ref, o_ref, st, kd, oraw):
    ts = q_ref.shape[0]
    si = pl.program_id(1)

    @pl.when(si == 0)
    def _():
        st[...] = jnp.zeros_like(st)

    ri = lax.broadcasted_iota(jnp.int32, (CHUNK, CHUNK), 0)
    ci = lax.broadcasted_iota(jnp.int32, (CHUNK, CHUNK), 1)
    tri = jnp.where(ci <= ri, 1.0, 0.0).astype(bf16)
    decs = []
    for c in range(ts // CHUNK):
        rs = slice(c * CHUNK, (c + 1) * CHUNK)
        la_hi, la_lo = _split_bf16(la_ref[rs, :])
        cum = _dot(tri, la_hi) + _dot(tri, la_lo)
        tot = cum[CHUNK - 1:CHUNK, :]
        kd[rs, :] = (k_ref[rs, :].astype(f32) * jnp.exp(tot - cum)).astype(bf16)
        decs.append(jnp.exp(tot))
    for c in range(ts // CHUNK):
        rs = slice(c * CHUNK, (c + 1) * CHUNK)
        for hd in range(GLA_HEADS):
            ks = slice(hd * GLA_DK, (hd + 1) * GLA_DK)
            vs = slice(hd * GLA_DV, (hd + 1) * GLA_DV)
            s_new = st[hd] * decs[c][:, ks] + _dot_tn(v_ref[rs, vs], kd[rs, ks])
            st[hd] = s_new
            oraw[rs, vs] = _dot_nt(q_ref[rs, ks], s_new.astype(bf16))
    for hd in range(GLA_HEADS):
        vs = slice(hd * GLA_DV, (hd + 1) * GLA_DV)
        o = _rmsnorm_rows(oraw[:, vs], g_ref[:, vs])
        o_ref[:, vs] = (o * og_ref[:, vs].astype(f32)).astype(bf16)


def _gla(q, k, v, la, og, norm_g):
    bsz, s, dqk = q.shape
    d = v.shape[-1]
    ts = TS_GLA
    row = lambda width: pl.BlockSpec((None, ts, width), lambda b, i: (b, i, 0))
    return pl.pallas_call(
        _gla_kernel,
        grid=(bsz, s // ts),
        in_specs=[row(dqk), row(dqk), row(d), row(dqk), row(d),
                  pl.BlockSpec(norm_g.shape, lambda b, i: (0, 0))],
        out_specs=row(d),
        out_shape=jax.ShapeDtypeStruct((bsz, s, d), bf16),
        scratch_shapes=[pltpu.VMEM((GLA_HEADS, GLA_DV, GLA_DK), f32),
                        pltpu.VMEM((ts, dqk), bf16), pltpu.VMEM((ts, d), f32)],
        compiler_params=_cparams(2),
        name="gla",
    )(q, k, v, la, og, norm_g)


def _post_kernel(ya_ref, ob_ref, ga_ref, gb_ref, h_ref, mod_ref, wa_ref, wb_ref, wo_ref, g2_ref,
                 wsgu_ref, wsd_ref, wrh_ref, wrl_ref, rb_ref,
                 hmid_ref, urow_ref, idx_ref, wts_ref, rank_ref, cnt_ref, cnt, lg):
    tm, d = h_ref.shape

    @pl.when(pl.program_id(1) % (G_TOK // tm) == 0)
    def _():
        cnt[...] = jnp.zeros_like(cnt)

    sub = tm // POST_SPLIT
    g1 = mod_ref[:, 2 * d:3 * d]
    sh2 = mod_ref[:, 3 * d:4 * d]
    sc2 = mod_ref[:, 4 * d:5 * d]
    g2 = mod_ref[:, 5 * d:6 * d]
    n_e = wrh_ref.shape[1]
    e_iota = lax.broadcasted_iota(jnp.int32, (n_e, sub), 0).astype(f32)
    ri = lax.broadcasted_iota(jnp.int32, (sub, sub), 0)
    ci = lax.broadcasted_iota(jnp.int32, (sub, sub), 1)
    upper = jnp.where(ri <= ci, 1.0, 0.0).astype(bf16)
    running = cnt[...]
    for hf in range(POST_SPLIT):
        rs = slice(hf * sub, (hf + 1) * sub)
        ya = _dot(ya_ref[rs, :], wa_ref[...])
        yb = _dot(ob_ref[rs, :], wb_ref[...])
        m = (ga_ref[rs, :].astype(f32) * ya + gb_ref[rs, :].astype(f32) * yb).astype(bf16)
        h1 = h_ref[rs, :] + g1 * _dot(m, wo_ref[...])
        u2 = _rmsnorm_rows(h1, g2_ref[...]) * (1.0 + sc2) + sh2
        u_hi, u_lo = _split_bf16(u2)

        gu = _dot(u_hi, wsgu_ref[...])
        dsh = wsd_ref.shape[0]
        act = (jax.nn.silu(gu[:, :dsh]) * gu[:, dsh:]).astype(bf16)
        hmid_ref[rs, :] = h1 + g2 * _dot(act, wsd_ref[...])

        uw = urow_ref.bitcast(jnp.int32)
        for jp in range(d // LANES // 2):
            word = _pack_bf16_pair(u_hi[:, (2 * jp) * LANES:(2 * jp + 1) * LANES],
                                   u_hi[:, (2 * jp + 1) * LANES:(2 * jp + 2) * LANES])
            uw[pl.ds(hf * sub * PACKED_SUBLANES + jp, sub, stride=PACKED_SUBLANES), :] = word

        lg[:, rs] = (_dot(u_hi, wrh_ref[...]) + _dot(u_lo, wrh_ref[...])
                     + _dot(u_hi, wrl_ref[...])).T

    for hf in range(POST_SPLIT):
        rs = slice(hf * sub, (hf + 1) * sub)
        scores = jax.nn.sigmoid(lg[:, rs])
        work = scores + rb_ref[...]
        onehots, idxs, svals = [], [], []
        for _ in range(TOP_K):
            mx = jnp.max(work, axis=0, keepdims=True)
            ik = jnp.min(jnp.where(work == mx, e_iota, float(n_e)), axis=0, keepdims=True)
            oh = e_iota == ik
            onehots.append(oh)
            idxs.append(ik.astype(jnp.int32))
            svals.append(jnp.sum(jnp.where(oh, scores, 0.0), axis=0, keepdims=True))
            work = jnp.where(oh, -jnp.inf, work)
        ssum = svals[0]
        for sv in svals[1:]:
            ssum = ssum + sv
        idx_ref[:, rs] = jnp.concatenate(idxs, axis=0)
        wts_ref[:, rs] = jnp.concatenate([sv / ssum * ROUTED_SCALE for sv in svals], axis=0)

        member = jnp.where(onehots[0], 1.0, 0.0)
        for oh in onehots[1:]:
            member = member + jnp.where(oh, 1.0, 0.0)
        incl = _dot(member.astype(bf16), upper)
        base = jnp.concatenate([running] * (sub // LANES), axis=1)
        pos = base + incl - member
        ranks = [jnp.sum(jnp.where(oh, pos, 0.0), axis=0, keepdims=True) for oh in onehots]
        rank_ref[:, rs] = jnp.concatenate(ranks, axis=0).astype(jnp.int32)
        running = running + jnp.broadcast_to(incl[:, sub - 1:sub], running.shape)
    cnt[...] = running
    cnt_ref[...] = running


def _post(ya_in, ob, ga, gb, h, mod, wa, wb, wo, g2, wsgu, wsd, wr_hi, wr_lo, rb):
    bsz, s, d = h.shape
    tm = TM_POST
    t = bsz * s
    nsb = s // tm
    tpg = G_TOK // tm
    n_groups = t // G_TOK
    row = pl.BlockSpec((None, tm, d), lambda b, i: (b, i, 0))
    full = lambda a: pl.BlockSpec(a.shape, lambda b, i: (0,) * a.ndim)
    tok = pl.BlockSpec((TOP_K, tm), lambda b, i: (0, b * nsb + i))
    return pl.pallas_call(
        _post_kernel,
        grid=(bsz, nsb),
        in_specs=[row, row, row, row, row,
                  pl.BlockSpec((None, 1, mod.shape[-1]), lambda b, i: (b, 0, 0)),
                  full(wa), full(wb), full(wo), full(g2), full(wsgu), full(wsd),
                  full(wr_hi), full(wr_lo), full(rb)],
        out_specs=[row,
                   pl.BlockSpec((tm * SUBLANES, LANES), lambda b, i: (b * nsb + i, 0)),
                   tok, tok, tok,
                   pl.BlockSpec((None, N_EXPERTS, LANES),
                                lambda b, i: ((b * nsb + i) // tpg, 0, 0))],
        out_shape=[jax.ShapeDtypeStruct((bsz, s, d), f32),
                   jax.ShapeDtypeStruct((t * SUBLANES, LANES), bf16),
                   jax.ShapeDtypeStruct((TOP_K, t), jnp.int32),
                   jax.ShapeDtypeStruct((TOP_K, t), f32),
                   jax.ShapeDtypeStruct((TOP_K, t), jnp.int32),
                   jax.ShapeDtypeStruct((n_groups, N_EXPERTS, LANES), f32)],
        scratch_shapes=[pltpu.VMEM((N_EXPERTS, LANES), f32), pltpu.VMEM((N_EXPERTS, tm), f32)],
        compiler_params=_cparams(2),
        name="post",
    )(ya_in, ob, ga, gb, h, mod, wa, wb, wo, g2, wsgu, wsd, wr_hi, wr_lo, rb)


def _dest_kernel(idx_ref, rank_ref, pstart_ref, o_ref):
    tn = idx_ref.shape[1]
    n_e = pstart_ref.shape[0]
    e_iota = lax.broadcasted_iota(jnp.int32, (n_e, tn), 0)
    rows = []
    for k in range(TOP_K):
        start = jnp.sum(jnp.where(e_iota == idx_ref[k:k + 1, :], pstart_ref[...], 0.0),
                        axis=0, keepdims=True)
        rows.append(start.astype(jnp.int32) + rank_ref[k:k + 1, :])
    o_ref[...] = jnp.concatenate(rows, axis=0)


def _dest(idx_t, rank_t, lstart_col):
    t = idx_t.shape[1]
    tn = G_TOK
    tok = pl.BlockSpec((TOP_K, tn), lambda i: (0, i))
    return pl.pallas_call(
        _dest_kernel,
        grid=(t // tn,),
        in_specs=[tok, tok, pl.BlockSpec((None, N_EXPERTS, 1), lambda i: (i, 0, 0))],
        out_specs=tok,
        out_shape=jax.ShapeDtypeStruct((TOP_K, t), jnp.int32),
        compiler_params=_cparams(1),
        name="dest",
    )(idx_t, rank_t, lstart_col)


TOK_UNROLL = 16


def _table_entry(tab_ref, i, c):
    return tab_ref[i * (TOK_UNROLL * TOP_K) + c]


def _disp_kernel(dest_ref, cnt_ref, gst_ref, pads_ref, padn_ref, nused_ref, u_ref, xg_ref,
                 xloc, zbuf, sem):
    g_tok = u_ref.shape[0]
    n_rows = g_tok * TOP_K
    g = pl.program_id(0)
    last = pl.num_programs(0) - 1
    slot = g % 2

    def drain(s):
        pltpu.make_async_copy(xloc.at[s], xg_ref.at[pl.ds(0, n_rows)], sem.at[s]).wait()

    @pl.when(g == 0)
    def _():
        zbuf[...] = jnp.zeros_like(zbuf)

    @pl.when(g >= 2)
    def _():
        drain(slot)

    xs = xloc.at[slot]
    uw = u_ref.bitcast(jnp.int32)
    xw2 = _packed_rows_view(xloc).at[slot]

    def fill(i, carry):
        for uu in range(TOK_UNROLL):
            t = i * TOK_UNROLL + uu
            row = uw[t]
            for k in range(TOP_K):
                slot_row = _table_entry(dest_ref, i, uu * TOP_K + k)
                off = pl.multiple_of(slot_row * PACKED_SUBLANES, PACKED_SUBLANES)
                xw2[pl.ds(off, PACKED_SUBLANES), :] = row
        return carry

    lax.fori_loop(0, g_tok // TOK_UNROLL, fill, 0)

    def seg(e, ls):
        c = cnt_ref[0, e]

        @pl.when(c > 0)
        def _():
            pltpu.make_async_copy(xs.at[pl.ds(ls, c)], xg_ref.at[pl.ds(gst_ref[0, e], c)],
                                  sem.at[slot]).start()

        return ls + c

    lax.fori_loop(0, N_EXPERTS, seg, 0)

    @pl.when(g == last)
    def _():
        def pad_copy(e):
            n = padn_ref[0, e]
            return pltpu.make_async_copy(zbuf.at[pl.ds(0, n)], xg_ref.at[pl.ds(pads_ref[0, e], n)],
                                         sem.at[2])

        def pad_start(e, carry):
            @pl.when(padn_ref[0, e] > 0)
            def _():
                pad_copy(e).start()
            return carry

        def pad_wait(e, carry):
            @pl.when(padn_ref[0, e] > 0)
            def _():
                pad_copy(e).wait()
            return carry

        n_blocks = xg_ref.shape[0] // R_EXP

        def tail_copy(i):
            return pltpu.make_async_copy(
                zbuf, xg_ref.at[pl.ds(pl.multiple_of(i * R_EXP, R_EXP), R_EXP)], sem.at[2])

        def tail_start(i, carry):
            tail_copy(i).start()
            return carry

        def tail_wait(i, carry):
            tail_copy(i).wait()
            return carry

        lax.fori_loop(0, N_EXPERTS, pad_start, 0)
        lax.fori_loop(nused_ref[0, 0], n_blocks, tail_start, 0)
        lax.fori_loop(0, N_EXPERTS, pad_wait, 0)
        lax.fori_loop(nused_ref[0, 0], n_blocks, tail_wait, 0)
        drain(1 - slot)
        drain(slot)


def _dispatch(dest_blk, cnt_tab, gst_tab, pad_start, pad_len, nused, urows, p_rows):
    t = urows.shape[0]
    n_groups = t // G_TOK
    assert n_groups >= 2
    smem_grp = lambda: pl.BlockSpec((None, 1, N_EXPERTS), lambda i: (i, 0, 0),
                                    memory_space=pltpu.SMEM)
    smem_all = lambda: pl.BlockSpec((1, N_EXPERTS), lambda i: (0, 0), memory_space=pltpu.SMEM)
    return pl.pallas_call(
        _disp_kernel,
        grid=(n_groups,),
        in_specs=[
            pl.BlockSpec((G_TOK * TOP_K,), lambda i: (i,), memory_space=pltpu.SMEM),
            smem_grp(), smem_grp(), smem_all(), smem_all(),
            pl.BlockSpec((1, 1), lambda i: (0, 0), memory_space=pltpu.SMEM),
            pl.BlockSpec((G_TOK, SUBLANES, LANES), lambda i: (i, 0, 0)),
        ],
        out_specs=pl.BlockSpec(memory_space=pl.ANY),
        out_shape=jax.ShapeDtypeStruct((p_rows, SUBLANES, LANES), bf16),
        scratch_shapes=[pltpu.VMEM((2, G_TOK * TOP_K, SUBLANES, LANES), bf16),
                        pltpu.VMEM((R_EXP, SUBLANES, LANES), bf16),
                        pltpu.SemaphoreType.DMA((3,))],
        compiler_params=_cparams(1),
        name="dispatch",
    )(dest_blk, cnt_tab, gst_tab, pad_start, pad_len, nused.reshape(1, 1), urows)


def _experts_kernel(be_ref, nused_ref, x_ref, wgu_ref, wd_ref, y_ref, wgu_b, wd_b):
    rows = x_ref.shape[0] // SUBLANES
    d = wgu_ref.shape[0]
    i = pl.program_id(0)

    @pl.when((i == 0) | (be_ref[i] != be_ref[jnp.maximum(i - 1, 0)]))
    def _():
        wgu_b[...] = wgu_ref[...].astype(bf16)
        wd_b[...] = wd_ref[...].astype(bf16)

    @pl.when(i < nused_ref[0])
    def _():
        xw = x_ref.bitcast(jnp.int32)
        yw = y_ref.bitcast(jnp.int32)
        sub = rows // EXP_SPLIT
        for hf in range(EXP_SPLIT):
            base = hf * sub * PACKED_SUBLANES
            gu = None
            for jp in range(d // LANES // 2):
                lo, hi = _unpack_bf16_pair(xw[pl.ds(base + jp, sub, stride=PACKED_SUBLANES), :])
                part = _dot(jnp.concatenate([lo, hi], axis=1),
                            wgu_b[2 * jp * LANES:(2 * jp + 2) * LANES, :])
                gu = part if gu is None else gu + part
            de = wd_b.shape[0]
            act = (jax.nn.silu(gu[:, :de]) * gu[:, de:]).astype(bf16)
            y = _dot(act, wd_b[...]).astype(bf16)
            for jp in range(d // LANES // 2):
                yw[pl.ds(base + jp, sub, stride=PACKED_SUBLANES), :] = _pack_bf16_pair(
                    y[:, (2 * jp) * LANES:(2 * jp + 1) * LANES],
                    y[:, (2 * jp + 1) * LANES:(2 * jp + 2) * LANES])

    @pl.when(i >= nused_ref[0])
    def _():
        y_ref[...] = jnp.zeros_like(y_ref)


def _experts(block_exp, nused, xg2, w_gu, w_down, layer):
    p8 = xg2.shape[0]
    _, n_e, d, de2 = w_gu.shape
    rb = R_EXP * SUBLANES
    nb = p8 // rb
    grid_spec = pltpu.PrefetchScalarGridSpec(
        num_scalar_prefetch=2,
        grid=(nb,),
        in_specs=[
            pl.BlockSpec((rb, LANES), lambda i, be, nu: (jnp.minimum(i, nu[0] - 1), 0)),
            pl.BlockSpec((None, None, d, de2), lambda i, be, nu: (layer, be[i], 0, 0)),
            pl.BlockSpec((None, None, de2 // 2, d), lambda i, be, nu: (layer, be[i], 0, 0)),
        ],
        out_specs=pl.BlockSpec((rb, LANES), lambda i, be, nu: (i, 0)),
        scratch_shapes=[pltpu.VMEM((d, de2), bf16), pltpu.VMEM((de2 // 2, d), bf16)],
    )
    return pl.pallas_call(
        _experts_kernel,
        grid_spec=grid_spec,
        out_shape=jax.ShapeDtypeStruct((p8, LANES), bf16),
        compiler_params=_cparams(1),
        name="experts",
    )(block_exp, nused, xg2, w_gu, w_down)


def _comb_kernel(dest_ref, wts_ref, cnt_ref, gst_ref, cntn_ref, gstn_ref, hmid_ref, mod_ref, fg_ref,
                 yg_ref, o_ref, yloc, acc, sem, *, final):
    g_tok, d = hmid_ref.shape
    n_rows = g_tok * TOP_K
    g = pl.program_id(0)
    n_groups = pl.num_programs(0)
    slot = g % 2

    def fetch(cnt_r, gst_r, s):
        def seg(e, ls):
            c = cnt_r[0, e]

            @pl.when(c > 0)
            def _():
                pltpu.make_async_copy(yg_ref.at[pl.ds(gst_r[0, e], c)],
                                      yloc.at[s, pl.ds(ls, c)], sem.at[s]).start()

            return ls + c

        lax.fori_loop(0, N_EXPERTS, seg, 0)

    @pl.when(g == 0)
    def _():
        fetch(cnt_ref, gst_ref, 0)

    @pl.when(g + 1 < n_groups)
    def _():
        fetch(cntn_ref, gstn_ref, 1 - slot)

    pltpu.make_async_copy(yg_ref.at[pl.ds(0, n_rows)], yloc.at[slot], sem.at[slot]).wait()

    yw2 = _packed_rows_view(yloc).at[slot]
    hi_mask = jnp.uint32(0xFFFF0000)

    def reduce(i, carry):
        for uu in range(TOK_UNROLL):
            t = i * TOK_UNROLL + uu
            a_lo = a_hi = None
            for k in range(TOP_K):
                slot_row = _table_entry(dest_ref, i, uu * TOP_K + k)
                off = pl.multiple_of(slot_row * PACKED_SUBLANES, PACKED_SUBLANES)
                bits = lax.bitcast_convert_type(yw2[pl.ds(off, PACKED_SUBLANES), :], jnp.uint32)
                w = _table_entry(wts_ref, i, uu * TOP_K + k)
                t_lo = lax.bitcast_convert_type(bits << 16, f32) * w
                t_hi = lax.bitcast_convert_type(bits & hi_mask, f32) * w
                a_lo = t_lo if a_lo is None else a_lo + t_lo
                a_hi = t_hi if a_hi is None else a_hi + t_hi
            row0 = pl.multiple_of(t * PACKED_SUBLANES, PACKED_SUBLANES)
            acc[0, pl.ds(row0, PACKED_SUBLANES), :] = a_lo
            acc[1, pl.ds(row0, PACKED_SUBLANES), :] = a_hi
        return carry

    lax.fori_loop(0, g_tok // TOK_UNROLL, reduce, 0)
    for j in range(d // LANES):
        cs = slice(j * LANES, (j + 1) * LANES)
        o_ref[:, cs] = hmid_ref[:, cs] + mod_ref[:, 5 * d + j * LANES:5 * d + (j + 1) * LANES] * \
            acc[j % 2, pl.ds(j // 2, g_tok, stride=PACKED_SUBLANES), :]
    if final:
        o_ref[...] = _rmsnorm_rows(o_ref[...], fg_ref[...])


def _combine(dest_blk, wts_blk, cnt_tab, gst_tab, hmid, mod, final_g, yg, final):
    t, d = hmid.shape
    n_groups = t // G_TOK
    gpb = n_groups // mod.shape[0]
    row = pl.BlockSpec((G_TOK, d), lambda i: (i, 0))
    smem_tok = lambda: pl.BlockSpec((G_TOK * TOP_K,), lambda i: (i,), memory_space=pltpu.SMEM)
    smem_cur = lambda: pl.BlockSpec((None, 1, N_EXPERTS), lambda i: (i, 0, 0),
                                    memory_space=pltpu.SMEM)
    smem_nxt = lambda: pl.BlockSpec((None, 1, N_EXPERTS),
                                    lambda i: (jnp.minimum(i + 1, n_groups - 1), 0, 0),
                                    memory_space=pltpu.SMEM)
    return pl.pallas_call(
        functools.partial(_comb_kernel, final=final),
        grid=(n_groups,),
        in_specs=[smem_tok(), smem_tok(), smem_cur(), smem_cur(), smem_nxt(), smem_nxt(), row,
                  pl.BlockSpec((None, 1, mod.shape[-1]), lambda i: (i // gpb, 0, 0)),
                  pl.BlockSpec(final_g.shape, lambda i: (0, 0)),
                  pl.BlockSpec(memory_space=pl.ANY)],
        out_specs=row,
        out_shape=jax.ShapeDtypeStruct((t, d), f32),
        scratch_shapes=[pltpu.VMEM((2, G_TOK * TOP_K, SUBLANES, LANES), bf16),
                        pltpu.VMEM((2, G_TOK * PACKED_SUBLANES, LANES), f32),
                        pltpu.SemaphoreType.DMA((2,))],
        compiler_params=_cparams(1),
        name="combine_final" if final else "combine",
    )(dest_blk, wts_blk, cnt_tab, gst_tab, cnt_tab, gst_tab, hmid, mod, final_g, yg)


def _tok_major(a_t):
    return a_t.T.reshape(-1)


def kernel(x, c, ada_w, ada_b, norm1_g, w_in, conv_w, conv_b, rg_wa, rg_ba, rg_wx, rg_bx, rg_lambda,
           gla_w_up, gla_b_up, gla_norm_g, w_branch_a, w_branch_b, w_out, norm2_g, router_w, router_b,
           exp_w_gu, exp_w_down, shared_w_gu, shared_w_down, final_g):
    bsz, s, d = x.shape
    n_layers = ada_w.shape[0]
    t = bsz * s
    dqk = GLA_HEADS * GLA_DK
    n_assign = t * TOP_K
    nb = -(-(n_assign + N_EXPERTS * (R_EXP - 1)) // R_EXP)
    p = nb * R_EXP
    n_groups = t // G_TOK

    mod_all = _ada(c, ada_w, ada_b)
    h = x
    for l in range(n_layers):
        mod = mod_all[l].reshape(bsz, 1, 6 * d)
        wl = w_in[l]
        lr0 = 2 * d + 2 * dqk + 2 * d
        w_main = jnp.concatenate([wl[:, :lr0], wl[:, lr0 + GLA_RANK:]], axis=1).astype(bf16)
        w_lr = jnp.pad(wl[:, lr0:lr0 + GLA_RANK], ((0, 0), (0, LANES - GLA_RANK))).astype(bf16)
        w_up = jnp.pad(gla_w_up[l], ((0, LANES - GLA_RANK), (0, 0))).astype(bf16)
        xr, gg, q, k, v, og, ga, gb, la = _in_proj(
            h, mod, norm1_g[l].reshape(1, d), w_main, w_lr, w_up, gla_b_up[l].reshape(1, dqk))

        wab = jnp.concatenate([rg_wa[l], rg_wx[l]], axis=-1).astype(bf16)
        ya_in = _rglru(xr, gg, conv_w[l], conv_b[l].reshape(1, d), wab,
                       rg_ba[l].reshape(1, d), rg_bx[l].reshape(1, d), rg_lambda[l].reshape(1, d))
        ob = _gla(q, k, v, la, og, gla_norm_g[l].reshape(1, d))

        wr_hi = router_w[l].astype(bf16)
        wr_lo = (router_w[l] - wr_hi.astype(f32)).astype(bf16)
        hmid, urows, idx_t, wts_t, rank_t, counts = _post(
            ya_in, ob, ga, gb, h, mod,
            w_branch_a[l].astype(bf16), w_branch_b[l].astype(bf16), w_out[l].astype(bf16),
            norm2_g[l].reshape(1, d), shared_w_gu[l].astype(bf16), shared_w_down[l].astype(bf16),
            wr_hi, wr_lo, router_b[l].reshape(N_EXPERTS, 1))

        cnt = counts[:, :, 0].astype(jnp.int32)
        lstart = jnp.cumsum(cnt, axis=1) - cnt
        ctot = jnp.sum(cnt, axis=0)
        padded = (ctot + R_EXP - 1) // R_EXP * R_EXP
        ends = jnp.cumsum(padded)
        pstart = ends - padded
        gstart = pstart[None, :] + jnp.cumsum(cnt, axis=0) - cnt
        blk_row = jnp.arange(nb, dtype=jnp.int32) * R_EXP
        block_exp = jnp.minimum(
            jnp.sum((ends[None, :] <= blk_row[:, None]).astype(jnp.int32), axis=1),
            N_EXPERTS - 1).astype(jnp.int32)
        nused = (ends[-1:] // R_EXP).astype(jnp.int32)
        cnt_tab = cnt.reshape(n_groups, 1, N_EXPERTS)
        gst_tab = gstart.astype(jnp.int32).reshape(n_groups, 1, N_EXPERTS)

        dest_t = _dest(idx_t, rank_t, lstart.astype(f32).reshape(n_groups, N_EXPERTS, 1))
        dest_blk = _tok_major(dest_t)
        xg = _dispatch(dest_blk, cnt_tab, gst_tab,
                       (pstart + ctot).astype(jnp.int32).reshape(1, N_EXPERTS),
                       (padded - ctot).astype(jnp.int32).reshape(1, N_EXPERTS),
                       nused, urows.reshape(t, SUBLANES, LANES), p)
        yg = _experts(block_exp, nused, xg.reshape(p * SUBLANES, LANES), exp_w_gu, exp_w_down, l)
        h = _combine(dest_blk, _tok_major(wts_t), cnt_tab, gst_tab, hmid.reshape(t, d), mod,
                     final_g.reshape(1, d), yg.reshape(p, SUBLANES, LANES),
                     l == n_layers - 1).reshape(bsz, s, d)
    return h
```

```python
import functools

import jax
import jax.numpy as jnp
from jax import lax
from jax.experimental import pallas as pl
from jax.experimental.pallas import tpu as pltpu

EPS = 1e-6
CHUNK = 64
CONV_W = 4
RG_C = 8.0
RNN_BLOCKS = 8
GLA_HEADS = 4
GLA_DK = 128
GLA_DV = 256
GLA_RANK = 16
GLA_TAU = 16.0
N_EXPERTS = 128
TOP_K = 8
D_EXPERT = 256
ROUTED_SCALE = 2.5

LANES = 128
SUBLANES = 8
PACKED_SUBLANES = 4
VMEM_LIMIT = 56 * 1024 * 1024

TM_IN = 512
TS_RNN = 512
TS_GLA = 512
TM_POST = 512
G_TOK = 1024
POST_SPLIT = 2
R_EXP = 1024
EXP_SPLIT = 2

bf16 = jnp.bfloat16
f32 = jnp.float32


def _cparams(n_axes):
    return pltpu.CompilerParams(
        dimension_semantics=("arbitrary",) * n_axes,
        vmem_limit_bytes=VMEM_LIMIT,
    )


def _dot(a, b):
    return jnp.dot(a, b, preferred_element_type=f32)


def _dot_nt(a, b):
    return lax.dot_general(a, b, (((1,), (1,)), ((), ())), preferred_element_type=f32)


def _dot_tn(a, b):
    return lax.dot_general(a, b, (((0,), (0,)), ((), ())), preferred_element_type=f32)


def _split_bf16(x):
    hi = x.astype(bf16)
    lo = (x - hi.astype(f32)).astype(bf16)
    return hi, lo


def _pack_bf16_pair(lo, hi):
    lo_bits = lax.bitcast_convert_type(lo.astype(f32), jnp.uint32) >> 16
    hi_bits = lax.bitcast_convert_type(hi.astype(f32), jnp.uint32) & jnp.uint32(0xFFFF0000)
    return lax.bitcast_convert_type(hi_bits | lo_bits, jnp.int32)


def _unpack_bf16_pair(word):
    bits = lax.bitcast_convert_type(word, jnp.uint32)
    lo = lax.bitcast_convert_type(bits << 16, f32).astype(bf16)
    hi = lax.bitcast_convert_type(bits & jnp.uint32(0xFFFF0000), f32).astype(bf16)
    return lo, hi


def _packed_rows_view(buf):
    two, n, sl, ln = buf.shape
    return (buf.reshape(two, n // 2, 2 * sl, ln).bitcast(jnp.int32)
            .reshape(two, n * PACKED_SUBLANES, ln))


def _rmsnorm_rows(x, g):
    return x * lax.rsqrt(jnp.mean(x * x, axis=-1, keepdims=True) + EPS) * g


def _log_sigmoid(x):
    return jnp.minimum(x, 0.0) - jnp.log1p(jnp.exp(-jnp.abs(x)))


def _softplus(x):
    return jnp.maximum(x, 0.0) + jnp.log1p(jnp.exp(-jnp.abs(x)))


def _ada_kernel(c_ref, w_ref, b_ref, o_ref):
    c_act = jax.nn.silu(c_ref[...])
    c_hi, c_lo = _split_bf16(c_act)
    w_hi, w_lo = _split_bf16(w_ref[...])
    acc = _dot(c_hi, w_hi) + _dot(c_lo, w_hi) + _dot(c_hi, w_lo)
    o_ref[...] = acc + b_ref[...]


def _ada(c, ada_w, ada_b):
    n_layers, d, n6 = ada_w.shape
    bsz = c.shape[0]
    tn = 512
    return pl.pallas_call(
        _ada_kernel,
        grid=(n_layers, n6 // tn),
        in_specs=[
            pl.BlockSpec((bsz, d), lambda l, j: (0, 0)),
            pl.BlockSpec((None, d, tn), lambda l, j: (l, 0, j)),
            pl.BlockSpec((None, 1, tn), lambda l, j: (l, 0, j)),
        ],
        out_specs=pl.BlockSpec((None, bsz, tn), lambda l, j: (l, 0, j)),
        out_shape=jax.ShapeDtypeStruct((n_layers, bsz, n6), f32),
        compiler_params=_cparams(2),
        name="ada_mod",
    )(c, ada_w, ada_b.reshape(n_layers, 1, n6))


def _in_kernel(h_ref, mod_ref, g_ref, w_ref, wlr_ref, wup_ref, bup_ref,
               xr_ref, gg_ref, q_ref, k_ref, v_ref, og_ref, ga_ref, gb_ref, la_ref, u_scr):
    d = h_ref.shape[-1]
    sh1 = mod_ref[:, 0:d]
    sc1 = mod_ref[:, d:2 * d]
    u = _rmsnorm_rows(h_ref[...], g_ref[...]) * (1.0 + sc1) + sh1
    u_scr[...] = u.astype(bf16)

    def proj(c0, width):
        return _dot(u_scr[...], w_ref[:, c0:c0 + width])

    cw = 512
    for c in range(d // cw):
        xr_ref[:, c * cw:(c + 1) * cw] = proj(c * cw, cw)
    for c in range(d // cw):
        y = proj(d + c * cw, cw)
        gg_ref[:, c * cw:(c + 1) * cw] = jax.nn.gelu(y, approximate=True).astype(bf16)
    dqk = GLA_HEADS * GLA_DK
    q_ref[...] = (proj(2 * d, dqk) * (GLA_DK ** -0.5)).astype(bf16)
    k_ref[...] = proj(2 * d + dqk, dqk).astype(bf16)
    base = 2 * d + 2 * dqk
    for c in range(d // cw):
        v_ref[:, c * cw:(c + 1) * cw] = proj(base + c * cw, cw).astype(bf16)
    base += d
    for c in range(d // cw):
        y = proj(base + c * cw, cw)
        og_ref[:, c * cw:(c + 1) * cw] = jax.nn.silu(y).astype(bf16)
    base += d
    for c in range(d // cw):
        y = proj(base + c * cw, cw)
        ga_ref[:, c * cw:(c + 1) * cw] = jax.nn.sigmoid(y).astype(bf16)
    base += d
    for c in range(d // cw):
        y = proj(base + c * cw, cw)
        gb_ref[:, c * cw:(c + 1) * cw] = jax.nn.sigmoid(y).astype(bf16)
    lr = _dot(u_scr[...], wlr_ref[...]).astype(bf16)
    logit = _dot(lr, wup_ref[...]) + bup_ref[...]
    la_ref[...] = _log_sigmoid(logit) * (1.0 / GLA_TAU)


def _in_proj(h, mod, norm_g, w_main, w_lr, w_up, b_up):
    bsz, s, d = h.shape
    tm = TM_IN
    dqk = GLA_HEADS * GLA_DK
    row = lambda width: pl.BlockSpec((None, tm, width), lambda b, i: (b, i, 0))
    full = lambda a: pl.BlockSpec(a.shape, lambda b, i: (0,) * a.ndim)
    out_shapes = [
        jax.ShapeDtypeStruct((bsz, s, d), f32),
        jax.ShapeDtypeStruct((bsz, s, d), bf16),
        jax.ShapeDtypeStruct((bsz, s, dqk), bf16),
        jax.ShapeDtypeStruct((bsz, s, dqk), bf16),
        jax.ShapeDtypeStruct((bsz, s, d), bf16),
        jax.ShapeDtypeStruct((bsz, s, d), bf16),
        jax.ShapeDtypeStruct((bsz, s, d), bf16),
        jax.ShapeDtypeStruct((bsz, s, d), bf16),
        jax.ShapeDtypeStruct((bsz, s, dqk), f32),
    ]
    return pl.pallas_call(
        _in_kernel,
        grid=(bsz, s // tm),
        in_specs=[
            row(d),
            pl.BlockSpec((None, 1, mod.shape[-1]), lambda b, i: (b, 0, 0)),
            full(norm_g),
            pl.BlockSpec(w_main.shape, lambda b, i: (0, 0), pipeline_mode=pl.Buffered(1)),
            full(w_lr), full(w_up), full(b_up),
        ],
        out_specs=[row(d), row(d), row(dqk), row(dqk), row(d), row(d), row(d), row(d), row(dqk)],
        out_shape=out_shapes,
        scratch_shapes=[pltpu.VMEM((tm, d), bf16)],
        compiler_params=_cparams(2),
        name="in_proj",
    )(h, mod, norm_g, w_main, w_lr, w_up, b_up)


def _rglru_kernel(xr_ref, gg_ref, cw_ref, cb_ref, wab_ref, ba_ref, bx_ref, lam_ref,
                  o_ref, xpad, hcar):
    ts, d = xr_ref.shape
    si = pl.program_id(1)

    @pl.when(si == 0)
    def _():
        xpad[0:SUBLANES, :] = jnp.zeros((SUBLANES, d), f32)
        hcar[...] = jnp.zeros_like(hcar)

    xpad[SUBLANES:SUBLANES + ts, :] = xr_ref[...]
    xc = cb_ref[...] + xpad[SUBLANES:SUBLANES + ts, :] * cw_ref[CONV_W - 1:CONV_W, :]
    for j in range(CONV_W - 1):
        off = SUBLANES - (CONV_W - 1) + j
        xc = xc + xpad[off:off + ts, :] * cw_ref[j:j + 1, :]
    xpad[0:SUBLANES, :] = xpad[ts:ts + SUBLANES, :]

    groups = ts // SUBLANES
    bw = d // RNN_BLOCKS
    sub = lax.broadcasted_iota(jnp.int32, (groups, SUBLANES, bw), 1)
    first_row = (lax.broadcasted_iota(jnp.int32, (ts, bw), 0) == 0) & (si == 0)
    neg_c_sp = (-RG_C) * _softplus(-lam_ref[...])
    xcb = xc.astype(bf16)
    for hb in range(RNN_BLOCKS):
        cs = slice(hb * bw, (hb + 1) * bw)
        x_h = xc[:, cs]
        g = _dot(xcb[:, cs], wab_ref[hb])
        r = jax.nn.sigmoid(g[:, :bw] + ba_ref[:, cs])
        i = jax.nn.sigmoid(g[:, bw:] + bx_ref[:, cs])
        log_a = r * neg_c_sp[:, cs]
        a = jnp.exp(log_a)
        mult = jnp.sqrt(1.0 - a * a)
        mult = jnp.where(first_row, 1.0, mult)
        bt = mult * (i * x_h)
        av = a.reshape(groups, SUBLANES, bw)
        bv = bt.reshape(groups, SUBLANES, bw)
        for sh in (1, 2, 4):
            a_prev = pltpu.roll(av, sh, 1)
            b_prev = pltpu.roll(bv, sh, 1)
            keep = sub >= sh
            bv = jnp.where(keep, av * b_prev + bv, bv)
            av = jnp.where(keep, av * a_prev, av)
        hin = hcar[0:1, cs]
        outs = []
        for gi in range(groups):
            hg = av[gi] * hin + bv[gi]
            outs.append(hg)
            hin = hg[SUBLANES - 1:SUBLANES, :]
        hcar[0:1, cs] = hin
        hs = jnp.concatenate(outs, axis=0)
        o_ref[:, cs] = (hs * gg_ref[:, cs].astype(f32)).astype(bf16)


def _rglru(xr, gg, conv_w, conv_b, wab, ba, bx, lam):
    bsz, s, d = xr.shape
    ts = TS_RNN
    row = pl.BlockSpec((None, ts, d), lambda b, i: (b, i, 0))
    full = lambda a: pl.BlockSpec(a.shape, lambda b, i: (0,) * a.ndim)
    return pl.pallas_call(
        _rglru_kernel,
        grid=(bsz, s // ts),
        in_specs=[row, row, full(conv_w), full(conv_b), full(wab), full(ba), full(bx), full(lam)],
        out_specs=row,
        out_shape=jax.ShapeDtypeStruct((bsz, s, d), bf16),
        scratch_shapes=[pltpu.VMEM((ts + SUBLANES, d), f32), pltpu.VMEM((SUBLANES, d), f32)],
        compiler_params=_cparams(2),
        name="rglru",
    )(xr, gg, conv_w, conv_b, wab, ba, bx, lam)


def _gla_kernel(q_ref, k_ref, v_ref, la_ref, og_ref, g_ref, o_ref, st, kd, oraw):
    ts = q_ref.shape[0]
    si = pl.program_id(1)

    @pl.when(si == 0)
    def _():
        st[...] = jnp.zeros_like(st)

    ri = lax.broadcasted_iota(jnp.int32, (CHUNK, CHUNK), 0)
    ci = lax.broadcasted_iota(jnp.int32, (CHUNK, CHUNK), 1)
    tri = jnp.where(ci <= ri, 1.0, 0.0).astype(bf16)
    decs = []
    for c in range(ts // CHUNK):
        rs = slice(c * CHUNK, (c + 1) * CHUNK)
        la_hi, la_lo = _split_bf16(la_ref[rs, :])
        cum = _dot(tri, la_hi) + _dot(tri, la_lo)
        tot = cum[CHUNK - 1:CHUNK, :]
        kd[rs, :] = (k_ref[rs, :].astype(f32) * jnp.exp(tot - cum)).astype(bf16)
        decs.append(jnp.exp(tot))
    for c in range(ts // CHUNK):
        rs = slice(c * CHUNK, (c + 1) * CHUNK)
        for hd in range(GLA_HEADS):
            ks = slice(hd * GLA_DK, (hd + 1) * GLA_DK)
            vs = slice(hd * GLA_DV, (hd + 1) * GLA_DV)
            s_new = st[hd] * decs[c][:, ks] + _dot_tn(v_ref[rs, vs], kd[rs, ks])
            st[hd] = s_new
            oraw[rs, vs] = _dot_nt(q_ref[rs, ks], s_new.astype(bf16))
    for hd in range(GLA_HEADS):
        vs = slice(hd * GLA_DV, (hd + 1) * GLA_DV)
        o = _rmsnorm_rows(oraw[:, vs], g_ref[:, vs])
        o_ref[:, vs] = (o * og_ref[:, vs].astype(f32)).astype(bf16)


def _gla(q, k, v, la, og, norm_g):
    bsz, s, dqk = q.shape
    d = v.shape[-1]
    ts = TS_GLA
    row = lambda width: pl.BlockSpec((None, ts, width), lambda b, i: (b, i, 0))
    return pl.pallas_call(
        _gla_kernel,
        grid=(bsz, s // ts),
        in_specs=[row(dqk), row(dqk), row(d), row(dqk), row(d),
                  pl.BlockSpec(norm_g.shape, lambda b, i: (0, 0))],
        out_specs=row(d),
        out_shape=jax.ShapeDtypeStruct((bsz, s, d), bf16),
        scratch_shapes=[pltpu.VMEM((GLA_HEADS, GLA_DV, GLA_DK), f32),
                        pltpu.VMEM((ts, dqk), bf16), pltpu.VMEM((ts, d), f32)],
        compiler_params=_cparams(2),
        name="gla",
    )(q, k, v, la, og, norm_g)


def _post_kernel(ya_ref, ob_ref, ga_ref, gb_ref, h_ref, mod_ref, wa_ref, wb_ref, wo_ref, g2_ref,
                 wsgu_ref, wsd_ref, wrh_ref, wrl_ref, rb_ref,
                 hmid_ref, urow_ref, idx_ref, wts_ref, rank_ref, cnt_ref, cnt, lg):
    tm, d = h_ref.shape

    @pl.when(pl.program_id(1) % (G_TOK // tm) == 0)
    def _():
        cnt[...] = jnp.zeros_like(cnt)

    sub = tm // POST_SPLIT
    g1 = mod_ref[:, 2 * d:3 * d]
    sh2 = mod_ref[:, 3 * d:4 * d]
    sc2 = mod_ref[:, 4 * d:5 * d]
    g2 = mod_ref[:, 5 * d:6 * d]
    n_e = wrh_ref.shape[1]
    e_iota = lax.broadcasted_iota(jnp.int32, (n_e, sub), 0).astype(f32)
    ri = lax.broadcasted_iota(jnp.int32, (sub, sub), 0)
    ci = lax.broadcasted_iota(jnp.int32, (sub, sub), 1)
    upper = jnp.where(ri <= ci, 1.0, 0.0).astype(bf16)
    running = cnt[...]
    for hf in range(POST_SPLIT):
        rs = slice(hf * sub, (hf + 1) * sub)
        ya = _dot(ya_ref[rs, :], wa_ref[...])
        yb = _dot(ob_ref[rs, :], wb_ref[...])
        m = (ga_ref[rs, :].astype(f32) * ya + gb_ref[rs, :].astype(f32) * yb).astype(bf16)
        h1 = h_ref[rs, :] + g1 * _dot(m, wo_ref[...])
        u2 = _rmsnorm_rows(h1, g2_ref[...]) * (1.0 + sc2) + sh2
        u_hi, u_lo = _split_bf16(u2)

        gu = _dot(u_hi, wsgu_ref[...])
        dsh = wsd_ref.shape[0]
        act = (jax.nn.silu(gu[:, :dsh]) * gu[:, dsh:]).astype(bf16)
        hmid_ref[rs, :] = h1 + g2 * _dot(act, wsd_ref[...])

        uw = urow_ref.bitcast(jnp.int32)
        for jp in range(d // LANES // 2):
            word = _pack_bf16_pair(u_hi[:, (2 * jp) * LANES:(2 * jp + 1) * LANES],
                                   u_hi[:, (2 * jp + 1) * LANES:(2 * jp + 2) * LANES])
            uw[pl.ds(hf * sub * PACKED_SUBLANES + jp, sub, stride=PACKED_SUBLANES), :] = word

        lg[:, rs] = (_dot(u_hi, wrh_ref[...]) + _dot(u_lo, wrh_ref[...])
                     + _dot(u_hi, wrl_ref[...])).T

    for hf in range(POST_SPLIT):
        rs = slice(hf * sub, (hf + 1) * sub)
        scores = jax.nn.sigmoid(lg[:, rs])
        work = scores + rb_ref[...]
        onehots, idxs, svals = [], [], []
        for _ in range(TOP_K):
            mx = jnp.max(work, axis=0, keepdims=True)
            ik = jnp.min(jnp.where(work == mx, e_iota, float(n_e)), axis=0, keepdims=True)
            oh = e_iota == ik
            onehots.append(oh)
            idxs.append(ik.astype(jnp.int32))
            svals.append(jnp.sum(jnp.where(oh, scores, 0.0), axis=0, keepdims=True))
            work = jnp.where(oh, -jnp.inf, work)
        ssum = svals[0]
        for sv in svals[1:]:
            ssum = ssum + sv
        idx_ref[:, rs] = jnp.concatenate(idxs, axis=0)
        wts_ref[:, rs] = jnp.concatenate([sv / ssum * ROUTED_SCALE for sv in svals], axis=0)

        member = jnp.where(onehots[0], 1.0, 0.0)
        for oh in onehots[1:]:
            member = member + jnp.where(oh, 1.0, 0.0)
        incl = _dot(member.astype(bf16), upper)
        base = jnp.concatenate([running] * (sub // LANES), axis=1)
        pos = base + incl - member
        ranks = [jnp.sum(jnp.where(oh, pos, 0.0), axis=0, keepdims=True) for oh in onehots]
        rank_ref[:, rs] = jnp.concatenate(ranks, axis=0).astype(jnp.int32)
        running = running + jnp.broadcast_to(incl[:, sub - 1:sub], running.shape)
    cnt[...] = running
    cnt_ref[...] = running


def _post(ya_in, ob, ga, gb, h, mod, wa, wb, wo, g2, wsgu, wsd, wr_hi, wr_lo, rb):
    bsz, s, d = h.shape
    tm = TM_POST
    t = bsz * s
    nsb = s // tm
    tpg = G_TOK // tm
    n_groups = t // G_TOK
    row = pl.BlockSpec((None, tm, d), lambda b, i: (b, i, 0))
    full = lambda a: pl.BlockSpec(a.shape, lambda b, i: (0,) * a.ndim)
    tok = pl.BlockSpec((TOP_K, tm), lambda b, i: (0, b * nsb + i))
    return pl.pallas_call(
        _post_kernel,
        grid=(bsz, nsb),
        in_specs=[row, row, row, row, row,
                  pl.BlockSpec((None, 1, mod.shape[-1]), lambda b, i: (b, 0, 0)),
                  full(wa), full(wb), full(wo), full(g2), full(wsgu), full(wsd),
                  full(wr_hi), full(wr_lo), full(rb)],
        out_specs=[row,
                   pl.BlockSpec((tm * SUBLANES, LANES), lambda b, i: (b * nsb + i, 0)),
                   tok, tok, tok,
                   pl.BlockSpec((None, N_EXPERTS, LANES),
                                lambda b, i: ((b * nsb + i) // tpg, 0, 0))],
        out_shape=[jax.ShapeDtypeStruct((bsz, s, d), f32),
                   jax.ShapeDtypeStruct((t * SUBLANES, LANES), bf16),
                   jax.ShapeDtypeStruct((TOP_K, t), jnp.int32),
                   jax.ShapeDtypeStruct((TOP_K, t), f32),
                   jax.ShapeDtypeStruct((TOP_K, t), jnp.int32),
                   jax.ShapeDtypeStruct((n_groups, N_EXPERTS, LANES), f32)],
        scratch_shapes=[pltpu.VMEM((N_EXPERTS, LANES), f32), pltpu.VMEM((N_EXPERTS, tm), f32)],
        compiler_params=_cparams(2),
        name="post",
    )(ya_in, ob, ga, gb, h, mod, wa, wb, wo, g2, wsgu, wsd, wr_hi, wr_lo, rb)


def _dest_kernel(idx_ref, rank_ref, pstart_ref, o_ref):
    tn = idx_ref.shape[1]
    n_e = pstart_ref.shape[0]
    e_iota = lax.broadcasted_iota(jnp.int32, (n_e, tn), 0)
    rows = []
    for k in range(TOP_K):
        start = jnp.sum(jnp.where(e_iota == idx_ref[k:k + 1, :], pstart_ref[...], 0.0),
                        axis=0, keepdims=True)
        rows.append(start.astype(jnp.int32) + rank_ref[k:k + 1, :])
    o_ref[...] = jnp.concatenate(rows, axis=0)


def _dest(idx_t, rank_t, lstart_col):
    t = idx_t.shape[1]
    tn = G_TOK
    tok = pl.BlockSpec((TOP_K, tn), lambda i: (0, i))
    return pl.pallas_call(
        _dest_kernel,
        grid=(t // tn,),
        in_specs=[tok, tok, pl.BlockSpec((None, N_EXPERTS, 1), lambda i: (i, 0, 0))],
        out_specs=tok,
        out_shape=jax.ShapeDtypeStruct((TOP_K, t), jnp.int32),
        compiler_params=_cparams(1),
        name="dest",
    )(idx_t, rank_t, lstart_col)


TOK_UNROLL = 16


def _table_entry(tab_ref, i, c):
    return tab_ref[i * (TOK_UNROLL * TOP_K) + c]


def _disp_kernel(dest_ref, cnt_ref, gst_ref, pads_ref, padn_ref, nused_ref, u_ref, xg_ref,
                 xloc, zbuf, sem):
    g_tok = u_ref.shape[0]
    n_rows = g_tok * TOP_K
    g = pl.program_id(0)
    last = pl.num_programs(0) - 1
    slot = g % 2

    def drain(s):
        pltpu.make_async_copy(xloc.at[s], xg_ref.at[pl.ds(0, n_rows)], sem.at[s]).wait()

    @pl.when(g == 0)
    def _():
        zbuf[...] = jnp.zeros_like(zbuf)

    @pl.when(g >= 2)
    def _():
        drain(slot)

    xs = xloc.at[slot]
    uw = u_ref.bitcast(jnp.int32)
    xw2 = _packed_rows_view(xloc).at[slot]

    def fill(i, carry):
        for uu in range(TOK_UNROLL):
            t = i * TOK_UNROLL + uu
            row = uw[t]
            for k in range(TOP_K):
                slot_row = _table_entry(dest_ref, i, uu * TOP_K + k)
                off = pl.multiple_of(slot_row * PACKED_SUBLANES, PACKED_SUBLANES)
                xw2[pl.ds(off, PACKED_SUBLANES), :] = row
        return carry

    lax.fori_loop(0, g_tok // TOK_UNROLL, fill, 0)

    def seg(e, ls):
        c = cnt_ref[0, e]

        @pl.when(c > 0)
        def _():
            pltpu.make_async_copy(xs.at[pl.ds(ls, c)], xg_ref.at[pl.ds(gst_ref[0, e], c)],
                                  sem.at[slot]).start()

        return ls + c

    lax.fori_loop(0, N_EXPERTS, seg, 0)

    @pl.when(g == last)
    def _():
        def pad_copy(e):
            n = padn_ref[0, e]
            return pltpu.make_async_copy(zbuf.at[pl.ds(0, n)], xg_ref.at[pl.ds(pads_ref[0, e], n)],
                                         sem.at[2])

        def pad_start(e, carry):
            @pl.when(padn_ref[0, e] > 0)
            def _():
                pad_copy(e).start()
            return carry

        def pad_wait(e, carry):
            @pl.when(padn_ref[0, e] > 0)
            def _():
                pad_copy(e).wait()
            return carry

        n_blocks = xg_ref.shape[0] // R_EXP

        def tail_copy(i):
            return pltpu.make_async_copy(
                zbuf, xg_ref.at[pl.ds(pl.multiple_of(i * R_EXP, R_EXP), R_EXP)], sem.at[2])

        def tail_start(i, carry):
            tail_copy(i).start()
            return carry

        def tail_wait(i, carry):
            tail_copy(i).wait()
            return carry

        lax.fori_loop(0, N_EXPERTS, pad_start, 0)
        lax.fori_loop(nused_ref[0, 0], n_blocks, tail_start, 0)
        lax.fori_loop(0, N_EXPERTS, pad_wait, 0)
        lax.fori_loop(nused_ref[0, 0], n_blocks, tail_wait, 0)
        drain(1 - slot)
        drain(slot)


def _dispatch(dest_blk, cnt_tab, gst_tab, pad_start, pad_len, nused, urows, p_rows):
    t = urows.shape[0]
    n_groups = t // G_TOK
    assert n_groups >= 2
    smem_grp = lambda: pl.BlockSpec((None, 1, N_EXPERTS), lambda i: (i, 0, 0),
                                    memory_space=pltpu.SMEM)
    smem_all = lambda: pl.BlockSpec((1, N_EXPERTS), lambda i: (0, 0), memory_space=pltpu.SMEM)
    return pl.pallas_call(
        _disp_kernel,
        grid=(n_groups,),
        in_specs=[
            pl.BlockSpec((G_TOK * TOP_K,), lambda i: (i,), memory_space=pltpu.SMEM),
            smem_grp(), smem_grp(), smem_all(), smem_all(),
            pl.BlockSpec((1, 1), lambda i: (0, 0), memory_space=pltpu.SMEM),
            pl.BlockSpec((G_TOK, SUBLANES, LANES), lambda i: (i, 0, 0)),
        ],
        out_specs=pl.BlockSpec(memory_space=pl.ANY),
        out_shape=jax.ShapeDtypeStruct((p_rows, SUBLANES, LANES), bf16),
        scratch_shapes=[pltpu.VMEM((2, G_TOK * TOP_K, SUBLANES, LANES), bf16),
                        pltpu.VMEM((R_EXP, SUBLANES, LANES), bf16),
                        pltpu.SemaphoreType.DMA((3,))],
        compiler_params=_cparams(1),
        name="dispatch",
    )(dest_blk, cnt_tab, gst_tab, pad_start, pad_len, nused.reshape(1, 1), urows)


X_RING = 3


def _experts_kernel(be_ref, nused_ref, xg_ref, wgu_ref, wd_ref, y_ref, wgu_b, wd_b, xbuf, xsem):
    rb = xbuf.shape[1]
    rows = rb // SUBLANES
    d = wgu_ref.shape[0]
    i = pl.program_id(0)
    nused = nused_ref[0]

    def fetch(j):
        return pltpu.make_async_copy(xg_ref.at[pl.ds(pl.multiple_of(j * rb, rb), rb)],
                                     xbuf.at[j % X_RING], xsem.at[j % X_RING])

    @pl.when(i == 0)
    def _():
        fetch(0).start()

        @pl.when(nused > 1)
        def _():
            fetch(1).start()

    @pl.when(i + 2 < nused)
    def _():
        fetch(i + 2).start()

    @pl.when(i < nused)
    def _():
        fetch(i).wait()

    @pl.when((i == 0) | (be_ref[i] != be_ref[jnp.maximum(i - 1, 0)]))
    def _():
        wgu_b[...] = wgu_ref[...].astype(bf16)
        wd_b[...] = wd_ref[...].astype(bf16)

    @pl.when(i < nused_ref[0])
    def _():
        xw = xbuf.at[i % X_RING].bitcast(jnp.int32)
        yw = y_ref.bitcast(jnp.int32)
        sub = rows // EXP_SPLIT
        for hf in range(EXP_SPLIT):
            base = hf * sub * PACKED_SUBLANES
            gu = None
            for jp in range(d // LANES // 2):
                lo, hi = _unpack_bf16_pair(xw[pl.ds(base + jp, sub, stride=PACKED_SUBLANES), :])
                part = _dot(jnp.concatenate([lo, hi], axis=1),
                            wgu_b[2 * jp * LANES:(2 * jp + 2) * LANES, :])
                gu = part if gu is None else gu + part
            de = wd_b.shape[0]
            act = (jax.nn.silu(gu[:, :de]) * gu[:, de:]).astype(bf16)
            y = _dot(act, wd_b[...]).astype(bf16)
            for jp in range(d // LANES // 2):
                yw[pl.ds(base + jp, sub, stride=PACKED_SUBLANES), :] = _pack_bf16_pair(
                    y[:, (2 * jp) * LANES:(2 * jp + 1) * LANES],
                    y[:, (2 * jp + 1) * LANES:(2 * jp + 2) * LANES])

    @pl.when(i >= nused_ref[0])
    def _():
        y_ref[...] = jnp.zeros_like(y_ref)


def _experts(block_exp, nused, xg2, w_gu, w_down, layer):
    p8 = xg2.shape[0]
    _, n_e, d, de2 = w_gu.shape
    rb = R_EXP * SUBLANES
    nb = p8 // rb
    grid_spec = pltpu.PrefetchScalarGridSpec(
        num_scalar_prefetch=2,
        grid=(nb,),
        in_specs=[
            pl.BlockSpec(memory_space=pl.ANY),
            pl.BlockSpec((None, None, d, de2), lambda i, be, nu: (layer, be[i], 0, 0)),
            pl.BlockSpec((None, None, de2 // 2, d), lambda i, be, nu: (layer, be[i], 0, 0)),
        ],
        out_specs=pl.BlockSpec((rb, LANES), lambda i, be, nu: (i, 0)),
        scratch_shapes=[pltpu.VMEM((d, de2), bf16), pltpu.VMEM((de2 // 2, d), bf16),
                        pltpu.VMEM((X_RING, rb, LANES), bf16), pltpu.SemaphoreType.DMA((X_RING,))],
    )
    return pl.pallas_call(
        _experts_kernel,
        grid_spec=grid_spec,
        out_shape=jax.ShapeDtypeStruct((p8, LANES), bf16),
        compiler_params=_cparams(1),
        name="experts",
    )(block_exp, nused, xg2, w_gu, w_down)


def _comb_kernel(dest_ref, wts_ref, cnt_ref, gst_ref, cntn_ref, gstn_ref, hmid_ref, mod_ref, fg_ref,
                 yg_ref, o_ref, yloc, acc, sem, *, final):
    g_tok, d = hmid_ref.shape
    n_rows = g_tok * TOP_K
    g = pl.program_id(0)
    n_groups = pl.num_programs(0)
    slot = g % 2

    def fetch(cnt_r, gst_r, s):
        def seg(e, ls):
            c = cnt_r[0, e]

            @pl.when(c > 0)
            def _():
                pltpu.make_async_copy(yg_ref.at[pl.ds(gst_r[0, e], c)],
                                      yloc.at[s, pl.ds(ls, c)], sem.at[s]).start()

            return ls + c

        lax.fori_loop(0, N_EXPERTS, seg, 0)

    @pl.when(g == 0)
    def _():
        fetch(cnt_ref, gst_ref, 0)

    @pl.when(g + 1 < n_groups)
    def _():
        fetch(cntn_ref, gstn_ref, 1 - slot)

    pltpu.make_async_copy(yg_ref.at[pl.ds(0, n_rows)], yloc.at[slot], sem.at[slot]).wait()

    yw2 = _packed_rows_view(yloc).at[slot]
    hi_mask = jnp.uint32(0xFFFF0000)

    def reduce(i, carry):
        for uu in range(TOK_UNROLL):
            t = i * TOK_UNROLL + uu
            a_lo = a_hi = None
            for k in range(TOP_K):
                slot_row = _table_entry(dest_ref, i, uu * TOP_K + k)
                off = pl.multiple_of(slot_row * PACKED_SUBLANES, PACKED_SUBLANES)
                bits = lax.bitcast_convert_type(yw2[pl.ds(off, PACKED_SUBLANES), :], jnp.uint32)
                w = _table_entry(wts_ref, i, uu * TOP_K + k)
                t_lo = lax.bitcast_convert_type(bits << 16, f32) * w
                t_hi = lax.bitcast_convert_type(bits & hi_mask, f32) * w
                a_lo = t_lo if a_lo is None else a_lo + t_lo
                a_hi = t_hi if a_hi is None else a_hi + t_hi
            row0 = pl.multiple_of(t * PACKED_SUBLANES, PACKED_SUBLANES)
            acc[0, pl.ds(row0, PACKED_SUBLANES), :] = a_lo
            acc[1, pl.ds(row0, PACKED_SUBLANES), :] = a_hi
        return carry

    lax.fori_loop(0, g_tok // TOK_UNROLL, reduce, 0)
    for j in range(d // LANES):
        cs = slice(j * LANES, (j + 1) * LANES)
        o_ref[:, cs] = hmid_ref[:, cs] + mod_ref[:, 5 * d + j * LANES:5 * d + (j + 1) * LANES] * \
            acc[j % 2, pl.ds(j // 2, g_tok, stride=PACKED_SUBLANES), :]
    if final:
        o_ref[...] = _rmsnorm_rows(o_ref[...], fg_ref[...])


def _combine(dest_blk, wts_blk, cnt_tab, gst_tab, hmid, mod, final_g, yg, final):
    t, d = hmid.shape
    n_groups = t // G_TOK
    gpb = n_groups // mod.shape[0]
    row = pl.BlockSpec((G_TOK, d), lambda i: (i, 0))
    smem_tok = lambda: pl.BlockSpec((G_TOK * TOP_K,), lambda i: (i,), memory_space=pltpu.SMEM)
    smem_cur = lambda: pl.BlockSpec((None, 1, N_EXPERTS), lambda i: (i, 0, 0),
                                    memory_space=pltpu.SMEM)
    smem_nxt = lambda: pl.BlockSpec((None, 1, N_EXPERTS),
                                    lambda i: (jnp.minimum(i + 1, n_groups - 1), 0, 0),
                                    memory_space=pltpu.SMEM)
    return pl.pallas_call(
        functools.partial(_comb_kernel, final=final),
        grid=(n_groups,),
        in_specs=[smem_tok(), smem_tok(), smem_cur(), smem_cur(), smem_nxt(), smem_nxt(), row,
                  pl.BlockSpec((None, 1, mod.shape[-1]), lambda i: (i // gpb, 0, 0)),
                  pl.BlockSpec(final_g.shape, lambda i: (0, 0)),
                  pl.BlockSpec(memory_space=pl.ANY)],
        out_specs=row,
        out_shape=jax.ShapeDtypeStruct((t, d), f32),
        scratch_shapes=[pltpu.VMEM((2, G_TOK * TOP_K, SUBLANES, LANES), bf16),
                        pltpu.VMEM((2, G_TOK * PACKED_SUBLANES, LANES), f32),
                        pltpu.SemaphoreType.DMA((2,))],
        compiler_params=_cparams(1),
        name="combine_final" if final else "combine",
    )(dest_blk, wts_blk, cnt_tab, gst_tab, cnt_tab, gst_tab, hmid, mod, final_g, yg)


def _tok_major(a_t):
    return a_t.T.reshape(-1)


def kernel(x, c, ada_w, ada_b, norm1_g, w_in, conv_w, conv_b, rg_wa, rg_ba, rg_wx, rg_bx, rg_lambda,
           gla_w_up, gla_b_up, gla_norm_g, w_branch_a, w_branch_b, w_out, norm2_g, router_w, router_b,
           exp_w_gu, exp_w_down, shared_w_gu, shared_w_down, final_g):
    bsz, s, d = x.shape
    n_layers = ada_w.shape[0]
    t = bsz * s
    dqk = GLA_HEADS * GLA_DK
    n_assign = t * TOP_K
    nb = -(-(n_assign + N_EXPERTS * (R_EXP - 1)) // R_EXP)
    p = nb * R_EXP
    n_groups = t // G_TOK

    mod_all = _ada(c, ada_w, ada_b)
    h = x
    for l in range(n_layers):
        mod = mod_all[l].reshape(bsz, 1, 6 * d)
        wl = w_in[l]
        lr0 = 2 * d + 2 * dqk + 2 * d
        w_main = jnp.concatenate([wl[:, :lr0], wl[:, lr0 + GLA_RANK:]], axis=1).astype(bf16)
        w_lr = jnp.pad(wl[:, lr0:lr0 + GLA_RANK], ((0, 0), (0, LANES - GLA_RANK))).astype(bf16)
        w_up = jnp.pad(gla_w_up[l], ((0, LANES - GLA_RANK), (0, 0))).astype(bf16)
        xr, gg, q, k, v, og, ga, gb, la = _in_proj(
            h, mod, norm1_g[l].reshape(1, d), w_main, w_lr, w_up, gla_b_up[l].reshape(1, dqk))

        wab = jnp.concatenate([rg_wa[l], rg_wx[l]], axis=-1).astype(bf16)
        ya_in = _rglru(xr, gg, conv_w[l], conv_b[l].reshape(1, d), wab,
                       rg_ba[l].reshape(1, d), rg_bx[l].reshape(1, d), rg_lambda[l].reshape(1, d))
        ob = _gla(q, k, v, la, og, gla_norm_g[l].reshape(1, d))

        wr_hi = router_w[l].astype(bf16)
        wr_lo = (router_w[l] - wr_hi.astype(f32)).astype(bf16)
        hmid, urows, idx_t, wts_t, rank_t, counts = _post(
            ya_in, ob, ga, gb, h, mod,
            w_branch_a[l].astype(bf16), w_branch_b[l].astype(bf16), w_out[l].astype(bf16),
            norm2_g[l].reshape(1, d), shared_w_gu[l].astype(bf16), shared_w_down[l].astype(bf16),
            wr_hi, wr_lo, router_b[l].reshape(N_EXPERTS, 1))

        cnt = counts[:, :, 0].astype(jnp.int32)
        lstart = jnp.cumsum(cnt, axis=1) - cnt
        ctot = jnp.sum(cnt, axis=0)
        padded = (ctot + R_EXP - 1) // R_EXP * R_EXP
        ends = jnp.cumsum(padded)
        pstart = ends - padded
        gstart = pstart[None, :] + jnp.cumsum(cnt, axis=0) - cnt
        blk_row = jnp.arange(nb, dtype=jnp.int32) * R_EXP
        block_exp = jnp.minimum(
            jnp.sum((ends[None, :] <= blk_row[:, None]).astype(jnp.int32), axis=1),
            N_EXPERTS - 1).astype(jnp.int32)
        nused = (ends[-1:] // R_EXP).astype(jnp.int32)
        cnt_tab = cnt.reshape(n_groups, 1, N_EXPERTS)
        gst_tab = gstart.astype(jnp.int32).reshape(n_groups, 1, N_EXPERTS)

        dest_t = _dest(idx_t, rank_t, lstart.astype(f32).reshape(n_groups, N_EXPERTS, 1))
        dest_blk = _tok_major(dest_t)
        xg = _dispatch(dest_blk, cnt_tab, gst_tab,
                       (pstart + ctot).astype(jnp.int32).reshape(1, N_EXPERTS),
                       (padded - ctot).astype(jnp.int32).reshape(1, N_EXPERTS),
                       nused, urows.reshape(t, SUBLANES, LANES), p)
        yg = _experts(block_exp, nused, xg.reshape(p * SUBLANES, LANES), exp_w_gu, exp_w_down, l)
        h = _combine(dest_blk, _tok_major(wts_t), cnt_tab, gst_tab, hmid.reshape(t, d), mod,
                     final_g.reshape(1, d), yg.reshape(p, SUBLANES, LANES),
                     l == n_layers - 1).reshape(bsz, s, d)
    return h
```
